```python
import jax
import jax.numpy as jnp
from jax import lax
import numpy as np

D_MODEL = 2048
BATCH = 32
SEQ = 256
DEPTH = 2
DEC_BATCH = 4
DEC_SEQ = 4096
PAST_LEN = 512

GRID_W = 64
N_AB_LAYERS = (DEPTH + 1) // 2
N_C_LAYERS = DEPTH // 2
RMS_EPS = 1e-6

RET_HEADS = 8
RET_DK = 64
RET_DV = 128
RET_CHUNK = 128

RWKV_HEADS = 16
RWKV_N = 64
RWKV_W = RWKV_HEADS * RWKV_N
W_RANK = 64
A_RANK = 64
G_RANK = 128
DECAY_SCALE = 0.606531
RWKV_LN_EPS = 64e-5
L2_EPS = 1e-12

ATT_HEADS = 16
ATT_KV_HEADS = 4
ATT_HD = 128
ATT_GROUP = ATT_HEADS // ATT_KV_HEADS
ROPE_AXIS_DIM = ATT_HD // 2
ROPE_THETA = 10000.0
Q_BLOCK = 128

FFN_HIDDEN = ((8 * D_MODEL + 3 * 256 - 1) // (3 * 256)) * 256

A_QK = RET_HEADS * RET_DK
A_V = RET_HEADS * RET_DV
A_IN = 2 * A_QK + 2 * A_V
B_IN = 3 * RWKV_W + G_RANK + 2 * W_RANK + 2 * A_RANK
AB_IN = A_IN + B_IN
AB_MIX = A_V + RWKV_W
KV_W = ATT_KV_HEADS * ATT_HD
C_MIX = ATT_HEADS * ATT_HD
C_IN = C_MIX + 2 * KV_W

kernel_name = 'hybrid_retention_rwkv7_gqa_diffusion_step'


def rms_norm(x, g=None, eps=RMS_EPS):
    xf = x.astype(jnp.float32)
    y = xf * lax.rsqrt(jnp.mean(xf * xf, axis=-1, keepdims=True) + eps)
    if g is not None:
        y = y * g.astype(jnp.float32)
    return y.astype(x.dtype)


def group_norm(x, w, b, eps):
    H, N = x.shape[-2:]
    xf = x.astype(jnp.float32)
    mu = jnp.mean(xf, axis=-1, keepdims=True)
    var = jnp.mean(jnp.square(xf - mu), axis=-1, keepdims=True)
    y = (xf - mu) * lax.rsqrt(var + eps)
    return (y * w.reshape(H, N).astype(jnp.float32) + b.reshape(H, N).astype(jnp.float32)).astype(x.dtype)


def adaln(cond, w, b):
    m = jax.nn.silu(cond) @ w + b
    return jnp.split(m[:, None, :], 6, axis=-1)


def flip_seq(t):
    return jnp.flip(t, axis=1)


def centred_conv3(x, w):
    xp = jnp.pad(x, ((0, 0), (1, 1), (0, 0)))
    return xp[:, :-2] * w[0] + xp[:, 1:-1] * w[1] + xp[:, 2:] * w[2]


def retention_scan(q, k, v, log_gamma, s0):
    B, L, H, DK = q.shape
    DV = v.shape[-1]
    C = RET_CHUNK
    nc = L // C
    idx = jnp.arange(C, dtype=jnp.float32)
    diff = idx[:, None] - idx[None, :]
    lg = log_gamma.astype(jnp.float32)
    decay_mask = jnp.where(diff >= 0, jnp.exp(lg[:, None, None] * jnp.maximum(diff, 0.0)), 0.0)
    xi = jnp.exp(lg[None, :] * (idx[:, None] + 1.0))[None, :, :, None]
    zeta = jnp.exp(lg[None, :] * (C - 1.0 - idx[:, None]))[None, :, :, None]
    gamma_c = jnp.exp(lg * C)[None, :, None, None]

    def to_chunks(t):
        return t.reshape(B, nc, C, H, t.shape[-1]).transpose(1, 0, 2, 3, 4).astype(jnp.float32)

    def step(s, inp):
        qi, ki, vi = inp
        scores = jnp.einsum('bnhd,bmhd->bhnm', qi, ki) * decay_mask
        inner = jnp.einsum('bhnm,bmhe->bnhe', scores, vi)
        cross = jnp.einsum('bnhd,bhde->bnhe', qi, s) * xi
        s_new = gamma_c * s + jnp.einsum('bmhd,bmhe->bhde', ki * zeta, vi)
        return s_new, inner + cross

    s_fin, out = lax.scan(step, s0.astype(jnp.float32), (to_chunks(q), to_chunks(k), to_chunks(v)))
    out = out.transpose(1, 0, 2, 3, 4).reshape(B, L, H, DV).astype(v.dtype)
    return out, s_fin


def rwkv7_scan(r, w, k, v, kk, a, s0):
    def step(s, inp):
        rt, wt, kt, vt, kkt, at = inp
        removed = jnp.einsum('bhij,bhj->bhi', s, -kkt)
        s = s * wt[:, :, None, :] + removed[..., None] * (kkt * at)[:, :, None, :] + vt[..., None] * kt[:, :, None, :]
        y = jnp.einsum('bhij,bhj->bhi', s, rt)
        return s, y

    xs = tuple(t.astype(jnp.float32).transpose(1, 0, 2, 3) for t in (r, w, k, v, kk, a))
    s_fin, y = lax.scan(step, s0.astype(jnp.float32), xs)
    return y.transpose(1, 0, 2, 3).astype(v.dtype), s_fin


def ab_mixer(h, j, P, init):
    B, L, _ = h.shape
    proj = h @ P['ab_w_in'][j]
    pa, pb = proj[..., :A_IN], proj[..., A_IN:]

    q, k, v, g = jnp.split(pa, [A_QK, 2 * A_QK, 2 * A_QK + A_V], axis=-1)
    q = q.reshape(B, L, RET_HEADS, RET_DK)
    k = k.reshape(B, L, RET_HEADS, RET_DK) * (RET_DK ** -0.5)
    v = v.reshape(B, L, RET_HEADS, RET_DV)
    log_gamma = -jnp.exp(P['ret_log_decay'][j].astype(jnp.float32))
    o_f, s_rf = retention_scan(q, k, v, log_gamma[0], init[0])
    o_b, s_rb = retention_scan(flip_seq(q), flip_seq(k), flip_seq(v), log_gamma[1], init[1])
    o_ret = rms_norm(o_f + flip_seq(o_b)).reshape(B, L, A_V) * jax.nn.silu(g)

    pb = centred_conv3(pb, P['rwkv_conv_w'][j])
    r, kb, vb, gc, wc, ac = jnp.split(pb, [RWKV_W, 2 * RWKV_W, 3 * RWKV_W, 3 * RWKV_W + G_RANK,
                                           3 * RWKV_W + G_RANK + 2 * W_RANK], axis=-1)

    def heads(t):
        return t.reshape(B, L, RWKV_HEADS, RWKV_N)

    r, kb, vb = heads(r), heads(kb), heads(vb)
    wc = wc.reshape(B, L, 2, W_RANK)
    ac = ac.reshape(B, L, 2, A_RANK)
    kkf = (kb * P['rwkv_k_k'][j].reshape(RWKV_HEADS, RWKV_N)).astype(jnp.float32)
    kk = (kkf / jnp.maximum(jnp.sqrt(jnp.sum(kkf * kkf, axis=-1, keepdims=True)), L2_EPS)).astype(kb.dtype)
    k_a = P['rwkv_k_a'][j].reshape(RWKV_HEADS, RWKV_N)
    r_k = P['rwkv_r_k'][j]
    ys, finals, bonuses = [], [], []
    for d in range(2):
        wd = heads(jnp.exp(-DECAY_SCALE * jax.nn.sigmoid(
            (P['rwkv_w0'][j, d] + jnp.tanh(wc[:, :, d]) @ P['rwkv_w_up'][j, d]).astype(jnp.float32))))
        ad = heads(jax.nn.sigmoid(P['rwkv_a0'][j, d] + ac[:, :, d] @ P['rwkv_a_up'][j, d]))
        kd = kb * (1.0 + (ad - 1.0) * k_a)
        seqs = (r, wd, kd, vb, kk, ad)
        if d == 1:
            seqs = tuple(flip_seq(t) for t in seqs)
        y, s_fin = rwkv7_scan(*seqs, init[2 + d])
        if d == 1:
            y = flip_seq(y)
        ys.append(y)
        finals.append(s_fin)
        bonuses.append(jnp.sum(r * kd * r_k, axis=-1, keepdims=True) * vb)
    y = group_norm(ys[0] + ys[1], P['rwkv_ln_w'][j], P['rwkv_ln_b'][j], RWKV_LN_EPS) + bonuses[0] + bonuses[1]
    g_b = jax.nn.sigmoid(gc) @ P['rwkv_g_up'][j]
    o_rwkv = y.reshape(B, L, RWKV_W) * g_b

    out = jnp.concatenate([o_ret, o_rwkv], axis=-1) @ P['ab_w_out'][j]
    return out, (s_rf, s_rb, finals[0], finals[1])


def gqa_qkv(h, w_in, q_g, k_g):
    B, L, _ = h.shape
    q, k, v = jnp.split(h @ w_in, [C_MIX, C_MIX + KV_W], axis=-1)
    q = rms_norm(q.reshape(B, L, ATT_HEADS, ATT_HD), q_g)
    k = rms_norm(k.reshape(B, L, ATT_KV_HEADS, ATT_HD), k_g)
    v = v.reshape(B, L, ATT_KV_HEADS, ATT_HD)
    return q, k, v


def axial_rope(x):
    L = x.shape[1]
    rows = L // GRID_W
    row = jnp.repeat(jnp.arange(rows, dtype=jnp.float32), GRID_W)
    col = jnp.tile(jnp.arange(GRID_W, dtype=jnp.float32), rows)
    freqs = jnp.power(ROPE_THETA, -jnp.arange(0, ROPE_AXIS_DIM, 2, dtype=jnp.float32) / ROPE_AXIS_DIM)

    def rotate(xa, pos):
        ang = pos[:, None] * freqs[None, :]
        cos = jnp.cos(ang)[None, :, None, :]
        sin = jnp.sin(ang)[None, :, None, :]
        x1, x2 = jnp.split(xa.astype(jnp.float32), 2, axis=-1)
        return jnp.concatenate([x1 * cos - x2 * sin, x2 * cos + x1 * sin], axis=-1)

    out = jnp.concatenate([rotate(x[..., :ROPE_AXIS_DIM], row), rotate(x[..., ROPE_AXIS_DIM:], col)], axis=-1)
    return out.astype(x.dtype)


def block_attention(q, k, v):
    B, Lq = q.shape[:2]
    nb = Lq // Q_BLOCK
    qb = q.reshape(B, nb, Q_BLOCK, ATT_KV_HEADS, ATT_GROUP, ATT_HD).transpose(1, 0, 2, 3, 4, 5)
    kf = k.astype(jnp.float32)
    vf = v.astype(jnp.float32)
    scale = ATT_HD ** -0.5

    def one_block(qi):
        s = jnp.einsum('bqhgd,bkhd->bhgqk', qi.astype(jnp.float32), kf) * scale
        p = jax.nn.softmax(s, axis=-1)
        return jnp.einsum('bhgqk,bkhd->bqhgd', p, vf).astype(q.dtype)

    o = lax.map(one_block, qb)
    return o.transpose(1, 0, 2, 3, 4, 5).reshape(B, Lq, C_MIX)


def swiglu(h, wg, wu, wd):
    return (jax.nn.silu(h @ wg) * (h @ wu)) @ wd


def trunk(x, cond, P, cache):
    ctx_mode = cache is None
    B = x.shape[0]
    new = {'ret_f': [], 'ret_b': [], 'rwkv_f': [], 'rwkv_b': [], 'k': [], 'v': []}
    for i in range(DEPTH):
        j = i // 2
        sh_m, sc_m, g_m, sh_f, sc_f, g_f = adaln(cond, P['ada_w'][i], P['ada_b'][i])
        ng = P['norm_g'][i]
        h = rms_norm(x, ng[0]) * (1.0 + sc_m) + sh_m
        if i % 2 == 0:
            if ctx_mode:
                z_ret = jnp.zeros((B, RET_HEADS, RET_DK, RET_DV), jnp.float32)
                z_rwkv = jnp.zeros((B, RWKV_HEADS, RWKV_N, RWKV_N), jnp.float32)
                init = (z_ret, z_ret, z_rwkv, z_rwkv)
            else:
                init = (cache['ret_f'][:, j], cache['ret_b'][:, j], cache['rwkv_f'][:, j], cache['rwkv_b'][:, j])
            y, fin = ab_mixer(h, j, P, init)
            if ctx_mode:
                for name, s in zip(('ret_f', 'ret_b', 'rwkv_f', 'rwkv_b'), fin):
                    new[name].append(s)
        else:
            q, k, v = gqa_qkv(h, P['c_w_in'][j], P['c_q_norm'][j], P['c_k_norm'][j])
            if ctx_mode:
                o = block_attention(q, k, v)
                new['k'].append(k)
                new['v'].append(v)
            else:
                keys = jnp.concatenate([cache['k'][:, j].astype(k.dtype), axial_rope(k)], axis=1)
                vals = jnp.concatenate([cache['v'][:, j].astype(v.dtype), v], axis=1)
                o = block_attention(axial_rope(q), keys, vals)
            y = o @ P['c_w_out'][j]
        x = x + g_m * rms_norm(y, ng[1])
        h = rms_norm(x, ng[2]) * (1.0 + sc_f) + sh_f
        x = x + g_f * rms_norm(swiglu(h, P['ffn_w_gate'][i], P['ffn_w_up'][i], P['ffn_w_down'][i]), ng[3])
    return x, new


def setup_inputs(seed: int = 0) -> dict:
    key = jax.random.key(seed)
    ks = iter(jax.random.split(key, 40))

    def nrm(shape, s=1.0):
        return s * jax.random.normal(next(ks), shape, jnp.float32)

    D = D_MODEL
    ret_base = jnp.log(-jnp.log(1.0 - jnp.exp2(-5.0 - jnp.arange(RET_HEADS, dtype=jnp.float32))))
    conv_base = jnp.array([[0.25], [0.5], [0.25]], jnp.float32)
    return {
        'x_prompt': nrm((BATCH, SEQ, D)),
        'x_sample': nrm((DEC_BATCH, DEC_SEQ, D)),
        'state_ret_fwd': nrm((DEC_BATCH, N_AB_LAYERS, RET_HEADS, RET_DK, RET_DV)),
        'state_ret_bwd': nrm((DEC_BATCH, N_AB_LAYERS, RET_HEADS, RET_DK, RET_DV)),
        'state_rwkv_fwd': nrm((DEC_BATCH, N_AB_LAYERS, RWKV_HEADS, RWKV_N, RWKV_N)),
        'state_rwkv_bwd': nrm((DEC_BATCH, N_AB_LAYERS, RWKV_HEADS, RWKV_N, RWKV_N)),
        'cache_k': nrm((DEC_BATCH, N_C_LAYERS, PAST_LEN, ATT_KV_HEADS, ATT_HD)),
        'cache_v': nrm((DEC_BATCH, N_C_LAYERS, PAST_LEN, ATT_KV_HEADS, ATT_HD)),
        'c': nrm((DEC_BATCH, D)),
        'c_ctx': nrm((D,)),
        'ada_w': nrm((DEPTH, D, 6 * D), 0.5 * D ** -0.5),
        'ada_b': nrm((DEPTH, 6 * D), 0.01),
        'norm_g': 1.0 + nrm((DEPTH, 4, D), 0.05),
        'ffn_w_gate': nrm((DEPTH, D, FFN_HIDDEN), D ** -0.5),
        'ffn_w_up': nrm((DEPTH, D, FFN_HIDDEN), D ** -0.5),
        'ffn_w_down': nrm((DEPTH, FFN_HIDDEN, D), FFN_HIDDEN ** -0.5),
        'ab_w_in': nrm((N_AB_LAYERS, D, AB_IN), D ** -0.5),
        'ab_w_out': nrm((N_AB_LAYERS, AB_MIX, D), AB_MIX ** -0.5),
        'ret_log_decay': ret_base + nrm((N_AB_LAYERS, 2, RET_HEADS), 0.05),
        'rwkv_conv_w': conv_base + nrm((N_AB_LAYERS, 3, B_IN), 0.1),
        'rwkv_w0': nrm((N_AB_LAYERS, 2, RWKV_W), 1.5) - 0.5,
        'rwkv_w_up': nrm((N_AB_LAYERS, 2, W_RANK, RWKV_W), 0.1),
        'rwkv_a0': nrm((N_AB_LAYERS, 2, RWKV_W), 0.5),
        'rwkv_a_up': nrm((N_AB_LAYERS, 2, A_RANK, RWKV_W), 0.1),
        'rwkv_g_up': nrm((N_AB_LAYERS, G_RANK, RWKV_W), G_RANK ** -0.5),
        'rwkv_k_k': 1.0 + nrm((N_AB_LAYERS, RWKV_W), 0.1),
        'rwkv_k_a': 1.0 + nrm((N_AB_LAYERS, RWKV_W), 0.1),
        'rwkv_r_k': nrm((N_AB_LAYERS, RWKV_HEADS, RWKV_N), 0.1),
        'rwkv_ln_w': 1.0 + nrm((N_AB_LAYERS, RWKV_W), 0.05),
        'rwkv_ln_b': nrm((N_AB_LAYERS, RWKV_W), 0.01),
        'c_w_in': nrm((N_C_LAYERS, D, C_IN), D ** -0.5),
        'c_w_out': nrm((N_C_LAYERS, C_MIX, D), C_MIX ** -0.5),
        'c_q_norm': 1.0 + nrm((N_C_LAYERS, ATT_HD), 0.05),
        'c_k_norm': 1.0 + nrm((N_C_LAYERS, ATT_HD), 0.05),
    }


def reference(x_prompt, x_sample, state_ret_fwd, state_ret_bwd, state_rwkv_fwd, state_rwkv_bwd, cache_k, cache_v,
              c, c_ctx, ada_w, ada_b, norm_g, ffn_w_gate, ffn_w_up, ffn_w_down, ab_w_in, ab_w_out, ret_log_decay,
              rwkv_conv_w, rwkv_w0, rwkv_w_up, rwkv_a0, rwkv_a_up, rwkv_g_up, rwkv_k_k, rwkv_k_a, rwkv_r_k,
              rwkv_ln_w, rwkv_ln_b, c_w_in, c_w_out, c_q_norm, c_k_norm):
    P = {
        'ada_w': ada_w, 'ada_b': ada_b, 'norm_g': norm_g,
        'ffn_w_gate': ffn_w_gate, 'ffn_w_up': ffn_w_up, 'ffn_w_down': ffn_w_down,
        'ab_w_in': ab_w_in, 'ab_w_out': ab_w_out, 'ret_log_decay': ret_log_decay,
        'rwkv_conv_w': rwkv_conv_w, 'rwkv_w0': rwkv_w0, 'rwkv_w_up': rwkv_w_up, 'rwkv_a0': rwkv_a0,
        'rwkv_a_up': rwkv_a_up, 'rwkv_g_up': rwkv_g_up, 'rwkv_k_k': rwkv_k_k, 'rwkv_k_a': rwkv_k_a,
        'rwkv_r_k': rwkv_r_k, 'rwkv_ln_w': rwkv_ln_w, 'rwkv_ln_b': rwkv_ln_b,
        'c_w_in': c_w_in, 'c_w_out': c_w_out, 'c_q_norm': c_q_norm, 'c_k_norm': c_k_norm,
    }
    y_prompt, st = trunk(x_prompt, c_ctx[None, :], P, None)
    cache = {'ret_f': state_ret_fwd, 'ret_b': state_ret_bwd, 'rwkv_f': state_rwkv_fwd, 'rwkv_b': state_rwkv_bwd,
             'k': cache_k, 'v': cache_v}
    y_sample, _ = trunk(x_sample, c, P, cache)
    dt = x_prompt.dtype
    new_ret_f = jnp.stack(st['ret_f'], axis=1).astype(dt)
    new_ret_b = jnp.stack(st['ret_b'], axis=1).astype(dt)
    new_rwkv_f = jnp.stack(st['rwkv_f'], axis=1).astype(dt)
    new_rwkv_b = jnp.stack(st['rwkv_b'], axis=1).astype(dt)
    new_k = jnp.stack(st['k'], axis=1).astype(dt)
    new_v = jnp.stack(st['v'], axis=1).astype(dt)
    return (y_prompt, y_sample, new_ret_f, new_ret_b, new_rwkv_f, new_rwkv_b, new_k, new_v)
```

```python
import functools

import jax
import jax.numpy as jnp
from jax import lax
from jax.experimental import pallas as pl
from jax.experimental.pallas import tpu as pltpu

F32 = jnp.float32
BF16 = jnp.bfloat16
HI = lax.Precision.HIGHEST

RMS_EPS = 1e-6
RWKV_LN_EPS = 64e-5
L2_EPS = 1e-12
DECAY_SCALE = 0.606531
ROPE_THETA = 10000.0
GRID_W = 64

RET_HEADS, RET_DK, RET_DV, RET_CHUNK = 8, 64, 128, 128
RWKV_HEADS, RWKV_N = 16, 64
RWKV_PAIRS = RWKV_HEADS // 2
RWKV_W = RWKV_HEADS * RWKV_N
RWKV_CHUNK = 64
ATT_HEADS, ATT_KV_HEADS, ATT_HD = 16, 4, 128
ATT_GROUP = ATT_HEADS // ATT_KV_HEADS

VMEM_LIMIT = 56 * 1024 * 1024


def _cp(*sem):
    return pltpu.CompilerParams(dimension_semantics=sem, vmem_limit_bytes=VMEM_LIMIT)


def _sigmoid(x):
    return 1.0 / (1.0 + jnp.exp(-x))


def _rms(x, eps=RMS_EPS):
    return x * lax.rsqrt(jnp.mean(x * x, axis=-1, keepdims=True) + eps)


def _dot(a, b, precision=None):
    return jnp.dot(a, b, preferred_element_type=F32, precision=precision)


def _dot_nt(a, b, precision=None):
    return lax.dot_general(a, b, (((1,), (1,)), ((), ())), preferred_element_type=F32, precision=precision)


def _dot_tn(a, b, precision=None):
    return lax.dot_general(a, b, (((0,), (0,)), ((), ())), preferred_element_type=F32, precision=precision)


def _ada_kernel(c_ref, w_ref, b_ref, o_ref):
    c = c_ref[...]
    o_ref[...] = _dot(c * _sigmoid(c), w_ref[...], HI) + b_ref[...]


def _ada(cond8, ada_w, ada_b):
    depth, d, n = ada_w.shape
    tn = 1024
    return pl.pallas_call(
        _ada_kernel,
        out_shape=jax.ShapeDtypeStruct((depth, 8, n), F32),
        grid=(depth, n // tn),
        in_specs=[pl.BlockSpec((8, d), lambda l, j: (0, 0)),
                  pl.BlockSpec((None, d, tn), lambda l, j: (l, 0, j)),
                  pl.BlockSpec((None, 1, tn), lambda l, j: (l, 0, j))],
        out_specs=pl.BlockSpec((None, 8, tn), lambda l, j: (l, 0, j)),
        compiler_params=_cp("parallel", "parallel"),
        name="ada_mod",
    )(cond8, ada_w, ada_b.reshape(depth, 1, n))


class _Rows:
    def __init__(self, n_p, l_p, n_s, l_s):
        self.n_p, self.l_p, self.n_s, self.l_s = n_p, l_p, n_s, l_s
        self.rows_p = n_p * l_p
        self.rows = self.rows_p + n_s * l_s

    def cond_row(self, i, tm):
        r0 = i * tm
        return jnp.where(r0 < self.rows_p, self.n_s, (r0 - self.rows_p) // self.l_s)


def _mod_spec(rows, tm, layer, piece, d):
    return pl.BlockSpec((None, None, None, 1, d),
                        lambda i, *_: (layer, rows.cond_row(i, tm), piece, 0, 0))


def _nmm_kernel(x_ref, g_ref, sh_ref, sc_ref, w_ref, o_ref, h_ref):
    @pl.when(pl.program_id(1) == 0)
    def _():
        h = _rms(x_ref[...]) * g_ref[...]
        h_ref[...] = (h * (1.0 + sc_ref[...]) + sh_ref[...]).astype(BF16)

    o_ref[...] = _dot(h_ref[...], w_ref[...]).astype(o_ref.dtype)


def _norm_mod_matmul(x, g, mod5, w, rows, layer, pieces, tm, tn, name):
    m, d = x.shape
    n = w.shape[1]
    return pl.pallas_call(
        _nmm_kernel,
        out_shape=jax.ShapeDtypeStruct((m, n), F32),
        grid=(m // tm, n // tn),
        in_specs=[pl.BlockSpec((tm, d), lambda i, j: (i, 0)),
                  pl.BlockSpec((1, d), lambda i, j: (0, 0)),
                  _mod_spec(rows, tm, layer, pieces[0], d),
                  _mod_spec(rows, tm, layer, pieces[1], d),
                  pl.BlockSpec((d, tn), lambda i, j: (0, j))],
        out_specs=pl.BlockSpec((tm, tn), lambda i, j: (i, j)),
        scratch_shapes=[pltpu.VMEM((tm, d), BF16)],
        compiler_params=_cp("parallel", "arbitrary"),
        name=name,
    )(x, g, mod5, mod5, w)


def _out_kernel(mix_ref, w_ref, x_ref, g_ref, gate_ref, o_ref):
    y = _dot(mix_ref[...], w_ref[...])
    o_ref[...] = x_ref[...] + gate_ref[...] * (_rms(y) * g_ref[...])


def _out_proj(mix, w, x, g, mod5, rows, layer, tm, name):
    m, d = x.shape
    k = mix.shape[1]
    return pl.pallas_call(
        _out_kernel,
        out_shape=jax.ShapeDtypeStruct((m, d), F32),
        grid=(m // tm,),
        in_specs=[pl.BlockSpec((tm, k), lambda i: (i, 0)),
                  pl.BlockSpec((k, d), lambda i: (0, 0)),
                  pl.BlockSpec((tm, d), lambda i: (i, 0)),
                  pl.BlockSpec((1, d), lambda i: (0, 0)),
                  _mod_spec(rows, tm, layer, 2, d)],
        out_specs=pl.BlockSpec((tm, d), lambda i: (i, 0)),
        compiler_params=_cp("parallel"),
        name=name,
    )(mix, w, x, g, mod5)


def _ffn_kernel(x_ref, g2_ref, sh_ref, sc_ref, wg_ref, wu_ref, wd_ref, g3_ref, gate_ref, o_ref, h_ref, acc_ref):
    j = pl.program_id(1)

    @pl.when(j == 0)
    def _():
        h = _rms(x_ref[...]) * g2_ref[...]
        h_ref[...] = (h * (1.0 + sc_ref[...]) + sh_ref[...]).astype(BF16)
        acc_ref[...] = jnp.zeros(acc_ref.shape, F32)

    h = h_ref[...]
    a = _dot(h, wg_ref[...])
    t = a * _sigmoid(a) * _dot(h, wu_ref[...])
    acc_ref[...] += _dot(t.astype(BF16), wd_ref[...])

    @pl.when(j == pl.num_programs(1) - 1)
    def _():
        o_ref[...] = x_ref[...] + gate_ref[...] * (_rms(acc_ref[...]) * g3_ref[...])


def _ffn(x, g2, g3, mod5, wg, wu, wd, rows, layer, tm, tf):
    m, d = x.shape
    f = wg.shape[1]
    return pl.pallas_call(
        _ffn_kernel,
        out_shape=jax.ShapeDtypeStruct((m, d), F32),
        grid=(m // tm, f // tf),
        in_specs=[pl.BlockSpec((tm, d), lambda i, j: (i, 0)),
                  pl.BlockSpec((1, d), lambda i, j: (0, 0)),
                  _mod_spec(rows, tm, layer, 3, d),
                  _mod_spec(rows, tm, layer, 4, d),
                  pl.BlockSpec((d, tf), lambda i, j: (0, j)),
                  pl.BlockSpec((d, tf), lambda i, j: (0, j)),
                  pl.BlockSpec((tf, d), lambda i, j: (j, 0)),
                  pl.BlockSpec((1, d), lambda i, j: (0, 0)),
                  _mod_spec(rows, tm, layer, 5, d)],
        out_specs=pl.BlockSpec((tm, d), lambda i, j: (i, 0)),
        scratch_shapes=[pltpu.VMEM((tm, d), BF16), pltpu.VMEM((tm, d), F32)],
        compiler_params=_cp("parallel", "arbitrary"),
        name=f"ffn{layer}",
    )(x, g2, mod5, mod5, wg, wu, wd, g3, mod5)


def _ret_kernel(ld_ref, qf_ref, kf_ref, vf_ref, qb_ref, kb_ref, vb_ref, s0_ref,
                of_ref, ob_ref, sfin_ref, s_scr, mask_scr, xi_scr, zeta_scr):
    c = RET_CHUNK
    b, i = pl.program_id(0), pl.program_id(1)

    @pl.when((b == 0) & (i == 0))
    def _():
        row = lax.broadcasted_iota(jnp.int32, (c, c), 0).astype(F32)
        col = lax.broadcasted_iota(jnp.int32, (c, c), 1).astype(F32)
        for d in range(2):
            diff = row - col if d == 0 else col - row
            pos = row if d == 0 else (c - 1.0) - row
            for h in range(RET_HEADS):
                lg = -jnp.exp(ld_ref[d, h])
                mask_scr[d, h] = jnp.where(diff >= 0, jnp.exp(lg * jnp.maximum(diff, 0.0)), 0.0)
                xi_scr[d, h] = jnp.exp(lg * (pos + 1.0))
                zeta_scr[d, h] = jnp.exp(lg * ((c - 1.0) - pos))

    @pl.when(i == 0)
    def _():
        s_scr[...] = s0_ref[...]

    for d, (q_ref, k_ref, v_ref, o_ref) in enumerate(((qf_ref, kf_ref, vf_ref, of_ref),
                                                      (qb_ref, kb_ref, vb_ref, ob_ref))):
        for h in range(RET_HEADS):
            q = q_ref[:, h * RET_DK:(h + 1) * RET_DK]
            k = k_ref[:, h * RET_DK:(h + 1) * RET_DK] * (RET_DK ** -0.5)
            v = v_ref[:, h * RET_DV:(h + 1) * RET_DV].astype(BF16)
            s = s_scr[d, h]
            qb = q.astype(BF16)
            scores = _dot_nt(qb, k.astype(BF16)) * mask_scr[d, h]
            inner = _dot(scores.astype(BF16), v)
            cross = _dot(qb, s.astype(BF16)) * xi_scr[d, h]
            o_ref[:, h * RET_DV:(h + 1) * RET_DV] = inner + cross
            kz = (k * zeta_scr[d, h][:, :RET_DK]).astype(BF16)
            gamma_c = jnp.exp(-jnp.exp(ld_ref[d, h]) * float(c))
            s_scr[d, h] = gamma_c * s + _dot_tn(kz, v)

    @pl.when(i == pl.num_programs(1) - 1)
    def _():
        sfin_ref[...] = s_scr[...]


def _retention(pa, ld_b, s0, row_off, n_seq, seq_len):
    c = RET_CHUNK
    nc = seq_len // c
    off = row_off // c
    a_qk, a_v = RET_HEADS * RET_DK, RET_HEADS * RET_DV

    def fwd(col):
        return lambda b, i: (off + b * nc + i, col)

    def bwd(col):
        return lambda b, i: (off + b * nc + (nc - 1 - i), col)

    st_spec = pl.BlockSpec((None, 2, RET_HEADS, RET_DK, RET_DV), lambda b, i: (b, 0, 0, 0, 0))
    return pl.pallas_call(
        _ret_kernel,
        out_shape=(jax.ShapeDtypeStruct((n_seq * seq_len, a_v), F32),
                   jax.ShapeDtypeStruct((n_seq * seq_len, a_v), F32),
                   jax.ShapeDtypeStruct((n_seq, 2, RET_HEADS, RET_DK, RET_DV), F32)),
        grid=(n_seq, nc),
        in_specs=[pl.BlockSpec((2, RET_HEADS, 1, c), lambda b, i: (0, 0, 0, 0)),
                  pl.BlockSpec((c, a_qk), fwd(0)), pl.BlockSpec((c, a_qk), fwd(1)), pl.BlockSpec((c, a_v), fwd(1)),
                  pl.BlockSpec((c, a_qk), bwd(0)), pl.BlockSpec((c, a_qk), bwd(1)), pl.BlockSpec((c, a_v), bwd(1)),
                  st_spec],
        out_specs=(pl.BlockSpec((c, a_v), lambda b, i: (b * nc + i, 0)),
                   pl.BlockSpec((c, a_v), lambda b, i: (b * nc + (nc - 1 - i), 0)),
                   st_spec),
        scratch_shapes=[pltpu.VMEM((2, RET_HEADS, RET_DK, RET_DV), F32),
                        pltpu.VMEM((2, RET_HEADS, c, c), F32),
                        pltpu.VMEM((2, RET_HEADS, c, c), F32),
                        pltpu.VMEM((2, RET_HEADS, c, c), F32)],
        compiler_params=_cp("arbitrary", "arbitrary"),
        name="retention",
    )(ld_b, pa, pa, pa, pa, pa, pa, s0)


def _pair_sum(x):
    r = lax.broadcasted_iota(jnp.int32, (128, 128), 0) // RWKV_N
    c = lax.broadcasted_iota(jnp.int32, (128, 128), 1) // RWKV_N
    return _dot(x, (r == c).astype(F32), HI)


def _prep_kernel(rows, tr, r_ref, k_ref, v_ref, s_ref, rp_ref, kp_ref, vp_ref, sp_ref,
                 rn_ref, kn_ref, vn_ref, sn_ref, cw_ref, w0_ref, wup_ref, a0_ref, aup_ref, gup_ref,
                 kk_ref, ka_ref, rk_ref,
                 r_o, v_o, kkn_o, gb_o, bonus_o, lw_o, kd_o, b_o):
    r0 = pl.program_id(0) * tr
    in_p = r0 < rows.rows_p
    pos = jnp.where(in_p, r0 % rows.l_p, (r0 - rows.rows_p) % rows.l_s)
    seq_len = jnp.where(in_p, rows.l_p, rows.l_s)
    has_prev = (pos != 0).astype(F32)
    has_next = (pos + tr != seq_len).astype(F32)
    rid = lax.broadcasted_iota(jnp.int32, (tr, 1), 0)

    def conv(cur_ref, prev_ref, next_ref, lo, hi):
        cur = cur_ref[...]
        w = cw_ref[:, lo:hi]
        up = jnp.where(rid == 0, prev_ref[7:8, :] * has_prev, pltpu.roll(cur, 1, 0))
        dn = jnp.where(rid == tr - 1, next_ref[0:1, :] * has_next, pltpu.roll(cur, tr - 1, 0))
        return up * w[0:1] + cur * w[1:2] + dn * w[2:3]

    w = RWKV_W
    r = conv(r_ref, rp_ref, rn_ref, 0, w)
    kb = conv(k_ref, kp_ref, kn_ref, w, 2 * w)
    vb = conv(v_ref, vp_ref, vn_ref, 2 * w, 3 * w)
    sm = conv(s_ref, sp_ref, sn_ref, 3 * w, 3 * w + 384)
    gc, wc, ac = sm[:, 0:128], sm[:, 128:256], sm[:, 256:384]

    gb = _dot(_sigmoid(gc), gup_ref[...], HI)
    twc = jnp.tanh(wc)
    lws, ads = [], []
    for d in range(2):
        lws.append(-DECAY_SCALE * _sigmoid(w0_ref[d:d + 1, :] + _dot(twc, wup_ref[d], HI)))
        ads.append(_sigmoid(a0_ref[d:d + 1, :] + _dot(ac, aup_ref[d], HI)))

    for p in range(RWKV_PAIRS):
        sl = slice(p * 128, (p + 1) * 128)
        rp, kp, vp = r[:, sl], kb[:, sl], vb[:, sl]
        kkf = kp * kk_ref[:, sl]
        kkn = kkf / jnp.maximum(jnp.sqrt(_pair_sum(kkf * kkf)), L2_EPS)
        r_o[p] = rp
        v_o[p] = vp
        kkn_o[p] = kkn
        gb_o[p] = gb[:, sl]
        bonus = jnp.zeros_like(rp)
        for d in range(2):
            ad = ads[d][:, sl]
            kd = kp * (1.0 + (ad - 1.0) * ka_ref[:, sl])
            lw_o[d, p] = lws[d][:, sl]
            kd_o[d, p] = kd
            b_o[d, p] = kkn * ad
            bonus = bonus + _pair_sum(rp * kd * rk_ref[:, sl]) * vp
        bonus_o[p] = bonus


def _rwkv_prep(pb, rows, cw, w0, wup, a0, aup, gup, kk, ka, rk, tr):
    m = pb.shape[0]
    w = RWKV_W
    nb8 = m // 8

    def cur(width, col):
        return pl.BlockSpec((tr, width), lambda i: (i, col))

    def prev(width, col):
        return pl.BlockSpec((8, width), lambda i: (jnp.maximum(i * (tr // 8) - 1, 0), col))

    def nxt(width, col):
        return pl.BlockSpec((8, width), lambda i: (jnp.minimum((i + 1) * (tr // 8), nb8 - 1), col))

    def full(a):
        nd = a.ndim
        return pl.BlockSpec(a.shape, lambda i: (0,) * nd)

    pm = jax.ShapeDtypeStruct((RWKV_PAIRS, m, 128), F32)
    pm2 = jax.ShapeDtypeStruct((2, RWKV_PAIRS, m, 128), F32)
    o1 = pl.BlockSpec((RWKV_PAIRS, tr, 128), lambda i: (0, i, 0))
    o2 = pl.BlockSpec((2, RWKV_PAIRS, tr, 128), lambda i: (0, 0, i, 0))
    small_col = (3 * w) // 384
    params = (cw, w0, wup, a0, aup, gup, kk, ka, rk)
    return pl.pallas_call(
        functools.partial(_prep_kernel, rows, tr),
        out_shape=(pm, pm, pm, pm, pm, pm2, pm2, pm2),
        grid=(m // tr,),
        in_specs=[cur(w, 0), cur(w, 1), cur(w, 2), cur(384, small_col),
                  prev(w, 0), prev(w, 1), prev(w, 2), prev(384, small_col),
                  nxt(w, 0), nxt(w, 1), nxt(w, 2), nxt(384, small_col)] + [full(a) for a in params],
        out_specs=(o1, o1, o1, o1, o1, o2, o2, o2),
        compiler_params=_cp("parallel"),
        name="rwkv_prep",
    )(pb, pb, pb, pb, pb, pb, pb, pb, pb, pb, pb, pb, *params)


def _scan_head(d, lw, r, v, kk, kd, b, t0):
    c = RWKV_CHUNK
    row = lax.broadcasted_iota(jnp.int32, (c, c), 0)
    col = lax.broadcasted_iota(jnp.int32, (c, c), 1)
    if d == 0:
        incl, strict, last = row >= col, row > col, c - 1
    else:
        incl, strict, last = row <= col, row < col, 0
    cum = _dot(incl.astype(F32), lw, HI)
    tot = cum[last:last + 1, :]
    e_in = jnp.exp(cum)
    e_neg = jnp.exp(-cum)
    e_end = jnp.exp(tot - cum)
    kkt = kk * jnp.exp(cum - lw)
    rt = r * e_in
    kh, bh = kd * e_neg, b * e_neg
    kbar, bbar = kd * e_end, b * e_end

    a_b = jnp.where(strict, _dot_nt(kkt, bh, HI), 0.0)
    a_k = jnp.where(strict, _dot_nt(kkt, kh, HI), 0.0)
    a_rb = jnp.where(incl, _dot_nt(rt, bh, HI), 0.0)
    a_rk = jnp.where(incl, _dot_nt(rt, kh, HI), 0.0)

    npow = -a_b
    minv = jnp.where(row == col, 1.0, 0.0) + npow
    for _ in range(5):
        npow = _dot(npow, npow, HI)
        minv = minv + _dot(minv, npow, HI)

    akv = _dot(a_k, v, HI)
    w1 = _dot(minv, kkt, HI)
    w2 = _dot(minv, akv, HI)
    q1 = rt - _dot(a_rb, w1, HI)
    y0 = _dot(a_rk, v, HI) - _dot(a_rb, w2, HI)
    g = jnp.where(row == col, jnp.exp(tot), 0.0) - _dot_tn(bbar, w1, HI)
    h = _dot_tn(kbar, v, HI) - _dot_tn(bbar, w2, HI)
    y = _dot(q1, t0, HI) + y0
    t1 = _dot(g, t0, HI) + h
    return y, t1


def _scan_kernel(rf, vf, kkf, lwf, kdf, bf, rb, vb, kkb, lwb, kdb, bb, s0_ref,
                 yf_ref, yb_ref, sfin_ref, t_scr):
    i = pl.program_id(1)
    n = RWKV_N

    @pl.when(i == 0)
    def _():
        for d in range(2):
            for h in range(RWKV_HEADS):
                t_scr[d, h] = s0_ref[d, h].T

    dirs = ((rf, vf, kkf, lwf, kdf, bf, yf_ref), (rb, vb, kkb, lwb, kdb, bb, yb_ref))

    def pair(p, carry):
        for d, (r_ref, v_ref, kk_ref, lw_ref, kd_ref, b_ref, y_ref) in enumerate(dirs):
            ys = []
            for half in range(2):
                sl = slice(half * n, (half + 1) * n)
                hd = 2 * p + half
                y, t1 = _scan_head(d, lw_ref[p][:, sl], r_ref[p][:, sl], v_ref[p][:, sl], kk_ref[p][:, sl],
                                   kd_ref[p][:, sl], b_ref[p][:, sl], t_scr[d, hd])
                t_scr[d, hd] = t1
                ys.append(y)
            y_ref[p] = jnp.concatenate(ys, axis=1)
        return carry

    lax.fori_loop(0, RWKV_PAIRS, pair, 0)

    @pl.when(i == pl.num_programs(1) - 1)
    def _():
        for d in range(2):
            for h in range(RWKV_HEADS):
                sfin_ref[d, h] = t_scr[d, h].T


def _rwkv_scan(r, v, kk, lw, kd, b, s0, row_off, n_seq, seq_len):
    c = RWKV_CHUNK
    nc = seq_len // c
    off = row_off // c

    def fwd3(b_, i):
        return (0, off + b_ * nc + i, 0)

    def bwd3(b_, i):
        return (0, off + b_ * nc + (nc - 1 - i), 0)

    def one(index):
        return pl.BlockSpec((RWKV_PAIRS, c, 128), index)

    def per_dir(d, index):
        return pl.BlockSpec((None, RWKV_PAIRS, c, 128), lambda b_, i: (d,) + index(b_, i))

    st_spec = pl.BlockSpec((None, 2, RWKV_HEADS, RWKV_N, RWKV_N), lambda b_, i: (b_, 0, 0, 0, 0))
    ysh = jax.ShapeDtypeStruct((RWKV_PAIRS, n_seq * seq_len, 128), F32)
    return pl.pallas_call(
        _scan_kernel,
        out_shape=(ysh, ysh, jax.ShapeDtypeStruct((n_seq, 2, RWKV_HEADS, RWKV_N, RWKV_N), F32)),
        grid=(n_seq, nc),
        in_specs=[one(fwd3), one(fwd3), one(fwd3), per_dir(0, fwd3), per_dir(0, fwd3), per_dir(0, fwd3),
                  one(bwd3), one(bwd3), one(bwd3), per_dir(1, bwd3), per_dir(1, bwd3), per_dir(1, bwd3),
                  st_spec],
        out_specs=(pl.BlockSpec((RWKV_PAIRS, c, 128), lambda b_, i: (0, b_ * nc + i, 0)),
                   pl.BlockSpec((RWKV_PAIRS, c, 128), lambda b_, i: (0, b_ * nc + (nc - 1 - i), 0)),
                   st_spec),
        scratch_shapes=[pltpu.VMEM((2, RWKV_HEADS, RWKV_N, RWKV_N), F32)],
        compiler_params=_cp("arbitrary", "arbitrary"),
        name="rwkv_scan",
    )(r, v, kk, lw, kd, b, r, v, kk, lw, kd, b, s0)


def _mix0_kernel(of_ref, ob_ref, g_ref, yf_ref, yb_ref, bonus_ref, gb_ref, lnw_ref, lnb_ref, o_ref):
    for h in range(RET_HEADS):
        sl = slice(h * RET_DV, (h + 1) * RET_DV)
        g = g_ref[:, sl]
        o_ref[:, sl] = (_rms(of_ref[:, sl] + ob_ref[:, sl]) * (g * _sigmoid(g))).astype(o_ref.dtype)
    base = RET_HEADS * RET_DV
    for p in range(RWKV_PAIRS):
        sl = slice(p * 128, (p + 1) * 128)
        y = yf_ref[p] + yb_ref[p]
        mu = _pair_sum(y) * (1.0 / RWKV_N)
        yc = y - mu
        var = _pair_sum(yc * yc) * (1.0 / RWKV_N)
        yn = yc * lax.rsqrt(var + RWKV_LN_EPS) * lnw_ref[:, sl] + lnb_ref[:, sl] + bonus_ref[p]
        o_ref[:, base + p * 128:base + (p + 1) * 128] = (yn * gb_ref[p]).astype(o_ref.dtype)


def _mix0(o_f, o_b, pa, y_f, y_b, bonus, gb, lnw, lnb, tr):
    m = o_f.shape[0]
    a_v = RET_HEADS * RET_DV
    row = pl.BlockSpec((tr, a_v), lambda i: (i, 0))
    pm = pl.BlockSpec((RWKV_PAIRS, tr, 128), lambda i: (0, i, 0))
    vec = pl.BlockSpec((1, RWKV_W), lambda i: (0, 0))
    return pl.pallas_call(
        _mix0_kernel,
        out_shape=jax.ShapeDtypeStruct((m, a_v + RWKV_W), BF16),
        grid=(m // tr,),
        in_specs=[row, row, pl.BlockSpec((tr, a_v), lambda i: (i, 2)), pm, pm, pm, pm, vec, vec],
        out_specs=pl.BlockSpec((tr, a_v + RWKV_W), lambda i: (i, 0)),
        compiler_params=_cp("parallel"),
        name="mix0",
    )(o_f, o_b, pa, y_f, y_b, bonus, gb, lnw, lnb)


def _rot_half(x):
    lane = lax.broadcasted_iota(jnp.int32, x.shape, 1) % 64
    return jnp.where(lane < 32, pltpu.roll(x, 96, 1), pltpu.roll(x, 32, 1))


def _qk_kernel(use_rope, qkv_ref, qg_ref, kg_ref, cos_ref, sin_ref, q_o, k_o, kn_o, v_o):
    hd = ATT_HD
    if use_rope:
        cos, sin = cos_ref[...], sin_ref[...]
    for h in range(ATT_HEADS):
        q = _rms(qkv_ref[:, h * hd:(h + 1) * hd]) * qg_ref[...]
        if use_rope:
            q = q * cos + _rot_half(q) * sin
        q_o[:, h * hd:(h + 1) * hd] = (q * (hd ** -0.5)).astype(q_o.dtype)
    kbase = ATT_HEADS * hd
    vbase = kbase + ATT_KV_HEADS * hd
    for h in range(ATT_KV_HEADS):
        k = _rms(qkv_ref[:, kbase + h * hd:kbase + (h + 1) * hd]) * kg_ref[...]
        kn_o[:, h * hd:(h + 1) * hd] = k
        if use_rope:
            k = k * cos + _rot_half(k) * sin
        k_o[:, h * hd:(h + 1) * hd] = k.astype(k_o.dtype)
    v_o[...] = qkv_ref[:, vbase:vbase + ATT_KV_HEADS * hd].astype(v_o.dtype)


def _qk_post(qkv, qg, kg, cos, sin, row_off, n_rows, seq_len, use_rope, tr):
    c_mix, kv_w = ATT_HEADS * ATT_HD, ATT_KV_HEADS * ATT_HD
    off = row_off // tr
    per_seq = seq_len // tr
    tab = pl.BlockSpec((tr, ATT_HD), lambda i: (i % per_seq, 0))
    vec = pl.BlockSpec((1, ATT_HD), lambda i: (0, 0))
    return pl.pallas_call(
        functools.partial(_qk_kernel, use_rope),
        out_shape=(jax.ShapeDtypeStruct((n_rows, c_mix), BF16),
                   jax.ShapeDtypeStruct((n_rows, kv_w), BF16),
                   jax.ShapeDtypeStruct((n_rows, kv_w), F32),
                   jax.ShapeDtypeStruct((n_rows, kv_w), BF16)),
        grid=(n_rows // tr,),
        in_specs=[pl.BlockSpec((tr, c_mix + 2 * kv_w), lambda i: (off + i, 0)), vec, vec, tab, tab],
        out_specs=(pl.BlockSpec((tr, c_mix), lambda i: (i, 0)),
                   pl.BlockSpec((tr, kv_w), lambda i: (i, 0)),
                   pl.BlockSpec((tr, kv_w), lambda i: (i, 0)),
                   pl.BlockSpec((tr, kv_w), lambda i: (i, 0))),
        compiler_params=_cp("parallel"),
        name="qk_post",
    )(qkv, qg, kg, cos, sin)


def _attn_kernel(q_ref, k_ref, v_ref, o_ref, m_scr, l_scr, acc_scr):
    j = pl.program_id(3)
    hd = ATT_HD

    @pl.when(j == 0)
    def _():
        m_scr[...] = jnp.full(m_scr.shape, -jnp.inf, F32)
        l_scr[...] = jnp.zeros(l_scr.shape, F32)
        acc_scr[...] = jnp.zeros(acc_scr.shape, F32)

    k = k_ref[...]
    v = v_ref[...]
    for g in range(ATT_GROUP):
        s = _dot_nt(q_ref[:, g * hd:(g + 1) * hd], k)
        m_old = m_scr[g]
        m_new = jnp.maximum(m_old, jnp.max(s, axis=-1, keepdims=True))
        p = jnp.exp(s - m_new)
        alpha = jnp.exp(m_old - m_new)
        l_scr[g] = alpha * l_scr[g] + jnp.sum(p, axis=-1, keepdims=True)
        acc_scr[g] = alpha * acc_scr[g] + _dot(p.astype(BF16), v)
        m_scr[g] = m_new

    @pl.when(j == pl.num_programs(3) - 1)
    def _():
        for g in range(ATT_GROUP):
            o_ref[:, g * hd:(g + 1) * hd] = (acc_scr[g] / l_scr[g]).astype(o_ref.dtype)


def _attention(q, k_all, v_all, n_seq, lq, tq, tk):
    lk = k_all.shape[1]
    nq = lq // tq
    gw = ATT_GROUP * ATT_HD
    return pl.pallas_call(
        _attn_kernel,
        out_shape=jax.ShapeDtypeStruct(q.shape, BF16),
        grid=(n_seq, ATT_KV_HEADS, nq, lk // tk),
        in_specs=[pl.BlockSpec((tq, gw), lambda b, h, i, j: (b * nq + i, h)),
                  pl.BlockSpec((None, tk, ATT_HD), lambda b, h, i, j: (b, j, h)),
                  pl.BlockSpec((None, tk, ATT_HD), lambda b, h, i, j: (b, j, h))],
        out_specs=pl.BlockSpec((tq, gw), lambda b, h, i, j: (b * nq + i, h)),
        scratch_shapes=[pltpu.VMEM((ATT_GROUP, tq, 1), F32), pltpu.VMEM((ATT_GROUP, tq, 1), F32),
                        pltpu.VMEM((ATT_GROUP, tq, ATT_HD), F32)],
        compiler_params=_cp("parallel", "parallel", "parallel", "arbitrary"),
        name="attention",
    )(q, k_all, v_all)


def _rope_tables(seq_len):
    t = jnp.arange(seq_len, dtype=jnp.int32)
    row = (t // GRID_W).astype(F32)
    col = (t % GRID_W).astype(F32)
    axis_dim = ATT_HD // 2
    freqs = jnp.power(ROPE_THETA, -jnp.arange(0, axis_dim, 2, dtype=F32) / axis_dim)
    ar, ac = row[:, None] * freqs[None, :], col[:, None] * freqs[None, :]
    cos = jnp.concatenate([jnp.cos(ar), jnp.cos(ar), jnp.cos(ac), jnp.cos(ac)], axis=-1)
    sin = jnp.concatenate([-jnp.sin(ar), jnp.sin(ar), -jnp.sin(ac), jnp.sin(ac)], axis=-1)
    return cos, sin


def _pad_cols(w, n):
    return jnp.pad(w, ((0, 0), (0, n - w.shape[1])))


def _pad_rank(w_up):
    z = jnp.zeros_like(w_up[0])
    return jnp.stack([jnp.concatenate([w_up[0], z], axis=0), jnp.concatenate([z, w_up[1]], axis=0)])


def kernel(x_prompt, x_sample, state_ret_fwd, state_ret_bwd, state_rwkv_fwd, state_rwkv_bwd, cache_k, cache_v,
           c, c_ctx, ada_w, ada_b, norm_g, ffn_w_gate, ffn_w_up, ffn_w_down, ab_w_in, ab_w_out, ret_log_decay,
           rwkv_conv_w, rwkv_w0, rwkv_w_up, rwkv_a0, rwkv_a_up, rwkv_g_up, rwkv_k_k, rwkv_k_a, rwkv_r_k,
           rwkv_ln_w, rwkv_ln_b, c_w_in, c_w_out, c_q_norm, c_k_norm):
    n_p, l_p, d = x_prompt.shape
    n_s, l_s, _ = x_sample.shape
    rows = _Rows(n_p, l_p, n_s, l_s)
    rp = rows.rows_p
    a_in = 2 * RET_HEADS * RET_DK + 2 * RET_HEADS * RET_DV
    w = RWKV_W

    x = jnp.concatenate([x_prompt.reshape(rp, d), x_sample.reshape(n_s * l_s, d)], axis=0)
    cond8 = jnp.zeros((8, d), F32).at[:n_s].set(c).at[n_s].set(c_ctx)
    mod5 = _ada(cond8, ada_w, ada_b).reshape(ada_w.shape[0], 8, 6, 1, d)

    tm = 512
    g0 = norm_g[0]
    w_a = ab_w_in[0][:, :a_in].astype(BF16)
    w_b = _pad_cols(ab_w_in[0][:, a_in:], 3584).astype(BF16)
    pa = _norm_mod_matmul(x, g0[0:1], mod5, w_a, rows, 0, (0, 1), tm, 512, "proj_ret")
    pb = _norm_mod_matmul(x, g0[0:1], mod5, w_b, rows, 0, (0, 1), tm, 512, "proj_rwkv")

    ld_b = jnp.broadcast_to(ret_log_decay[0][:, :, None, None], (2, RET_HEADS, 1, RET_CHUNK))
    zr = jnp.zeros((n_p, 2, RET_HEADS, RET_DK, RET_DV), F32)
    sr = jnp.stack([state_ret_fwd[:, 0], state_ret_bwd[:, 0]], axis=1)
    of_p, ob_p, ret_fin = _retention(pa, ld_b, zr, 0, n_p, l_p)
    of_s, ob_s, _ = _retention(pa, ld_b, sr, rp, n_s, l_s)

    cw = _pad_cols(rwkv_conv_w[0], 3584)
    r_, v_, kk_, gb_, bonus_, lw_, kd_, b_ = _rwkv_prep(
        pb, rows, cw, rwkv_w0[0], _pad_rank(rwkv_w_up[0]), rwkv_a0[0], _pad_rank(rwkv_a_up[0]), rwkv_g_up[0],
        rwkv_k_k[0][None], rwkv_k_a[0][None], rwkv_r_k[0].reshape(1, w), 256)
    zw = jnp.zeros((n_p, 2, RWKV_HEADS, RWKV_N, RWKV_N), F32)
    sw = jnp.stack([state_rwkv_fwd[:, 0], state_rwkv_bwd[:, 0]], axis=1)
    yf_p, yb_p, rwkv_fin = _rwkv_scan(r_, v_, kk_, lw_, kd_, b_, zw, 0, n_p, l_p)
    yf_s, yb_s, _ = _rwkv_scan(r_, v_, kk_, lw_, kd_, b_, sw, rp, n_s, l_s)

    o_f = jnp.concatenate([of_p, of_s], axis=0)
    o_b = jnp.concatenate([ob_p, ob_s], axis=0)
    y_f = jnp.concatenate([yf_p, yf_s], axis=1)
    y_b = jnp.concatenate([yb_p, yb_s], axis=1)
    mix = _mix0(o_f, o_b, pa, y_f, y_b, bonus_, gb_, rwkv_ln_w[0][None], rwkv_ln_b[0][None], 256)
    x = _out_proj(mix, ab_w_out[0].astype(BF16), x, g0[1:2], mod5, rows, 0, tm, "out_proj0")
    x = _ffn(x, g0[2:3], g0[3:4], mod5, ffn_w_gate[0].astype(BF16), ffn_w_up[0].astype(BF16),
             ffn_w_down[0].astype(BF16), rows, 0, tm, 512)

    g1 = norm_g[1]
    kv_w = ATT_KV_HEADS * ATT_HD
    qkv = _norm_mod_matmul(x, g1[0:1], mod5, c_w_in[0].astype(BF16), rows, 1, (0, 1), tm, 512, "proj_qkv")
    cos, sin = _rope_tables(l_s)
    qg, kg = c_q_norm[0][None], c_k_norm[0][None]
    q_p, k_p, kn_p, v_p = _qk_post(qkv, qg, kg, cos, sin, 0, rp, l_p, False, 256)
    q_s, k_s, _, v_s = _qk_post(qkv, qg, kg, cos, sin, rp, n_s * l_s, l_s, True, 256)
    o_p = _attention(q_p, k_p.reshape(n_p, l_p, kv_w), v_p.reshape(n_p, l_p, kv_w), n_p, l_p, l_p, l_p)
    past = cache_k.shape[2]
    k_all = jnp.concatenate([cache_k[:, 0].reshape(n_s, past, kv_w).astype(BF16), k_s.reshape(n_s, l_s, kv_w)], axis=1)
    v_all = jnp.concatenate([cache_v[:, 0].reshape(n_s, past, kv_w).astype(BF16), v_s.reshape(n_s, l_s, kv_w)], axis=1)
    o_s = _attention(q_s, k_all, v_all, n_s, l_s, 512, 512)
    o = jnp.concatenate([o_p, o_s], axis=0)
    x = _out_proj(o, c_w_out[0].astype(BF16), x, g1[1:2], mod5, rows, 1, tm, "out_proj1")
    x = _ffn(x, g1[2:3], g1[3:4], mod5, ffn_w_gate[1].astype(BF16), ffn_w_up[1].astype(BF16),
             ffn_w_down[1].astype(BF16), rows, 1, tm, 512)

    y_prompt = x[:rp].reshape(n_p, l_p, d)
    y_sample = x[rp:].reshape(n_s, l_s, d)
    new_k = kn_p.reshape(n_p, 1, l_p, ATT_KV_HEADS, ATT_HD)
    new_v = qkv[:rp, ATT_HEADS * ATT_HD + kv_w:].reshape(n_p, 1, l_p, ATT_KV_HEADS, ATT_HD)
    return (y_prompt, y_sample, ret_fin[:, 0][:, None], ret_fin[:, 1][:, None],
            rwkv_fin[:, 0][:, None], rwkv_fin[:, 1][:, None], new_k, new_v)
```

```python
import functools

import jax
import jax.numpy as jnp
from jax import lax
from jax.experimental import pallas as pl
from jax.experimental.pallas import tpu as pltpu

F32 = jnp.float32
BF16 = jnp.bfloat16
HI = lax.Precision.HIGHEST

RMS_EPS = 1e-6
RWKV_LN_EPS = 64e-5
L2_EPS = 1e-12
DECAY_SCALE = 0.606531
ROPE_THETA = 10000.0
GRID_W = 64
LOG2E = 1.4426950408889634

RET_HEADS, RET_DK, RET_DV, RET_CHUNK = 8, 64, 128, 128
RWKV_HEADS, RWKV_N = 16, 64
RWKV_PAIRS = RWKV_HEADS // 2
RWKV_W = RWKV_HEADS * RWKV_N
RWKV_CHUNK = 64
SCAN_PASSES_P, SCAN_PASSES_S = 1, 1
ATT_HEADS, ATT_KV_HEADS, ATT_HD = 16, 4, 128
ATT_GROUP = ATT_HEADS // ATT_KV_HEADS

VMEM_LIMIT = 56 * 1024 * 1024


def _cp(*sem):
    return pltpu.CompilerParams(dimension_semantics=sem, vmem_limit_bytes=VMEM_LIMIT)


def _sigmoid(x):
    return 1.0 / (1.0 + jnp.exp(-x))


def _rms(x, eps=RMS_EPS):
    return x * lax.rsqrt(jnp.mean(x * x, axis=-1, keepdims=True) + eps)


def _dot(a, b, precision=None):
    return jnp.dot(a, b, preferred_element_type=F32, precision=precision)


def _dot_nt(a, b, precision=None):
    return lax.dot_general(a, b, (((1,), (1,)), ((), ())), preferred_element_type=F32, precision=precision)


def _dot_tn(a, b, precision=None):
    return lax.dot_general(a, b, (((0,), (0,)), ((), ())), preferred_element_type=F32, precision=precision)


def _ada_kernel(c_ref, w_ref, b_ref, o_ref):
    c = c_ref[...]
    o_ref[...] = _dot(c * _sigmoid(c), w_ref[...], HI) + b_ref[...]


def _ada(cond8, ada_w, ada_b):
    depth, d, n = ada_w.shape
    tn = 1024
    return pl.pallas_call(
        _ada_kernel,
        out_shape=jax.ShapeDtypeStruct((depth, 8, n), F32),
        grid=(depth, n // tn),
        in_specs=[pl.BlockSpec((8, d), lambda l, j: (0, 0)),
                  pl.BlockSpec((None, d, tn), lambda l, j: (l, 0, j)),
                  pl.BlockSpec((None, 1, tn), lambda l, j: (l, 0, j))],
        out_specs=pl.BlockSpec((None, 8, tn), lambda l, j: (l, 0, j)),
        compiler_params=_cp("parallel", "parallel"),
        name="ada_mod",
    )(cond8, ada_w, ada_b.reshape(depth, 1, n))


class _Rows:
    def __init__(self, n_p, l_p, n_s, l_s):
        self.n_p, self.l_p, self.n_s, self.l_s = n_p, l_p, n_s, l_s
        self.rows_p = n_p * l_p
        self.rows = self.rows_p + n_s * l_s

    def cond_row(self, i, tm):
        r0 = i * tm
        return jnp.where(r0 < self.rows_p, self.n_s, (r0 - self.rows_p) // self.l_s)


def _mod_spec(rows, tm, layer, piece, d):
    return pl.BlockSpec((None, None, None, 1, d),
                        lambda i, *_: (layer, rows.cond_row(i, tm), piece, 0, 0))


def _nmm_kernel(x_ref, g_ref, sh_ref, sc_ref, w_ref, o_ref, h_ref):
    @pl.when(pl.program_id(1) == 0)
    def _():
        h = _rms(x_ref[...]) * g_ref[...]
        h_ref[...] = (h * (1.0 + sc_ref[...]) + sh_ref[...]).astype(BF16)

    o_ref[...] = _dot(h_ref[...], w_ref[...]).astype(o_ref.dtype)


def _norm_mod_matmul(x, g, mod5, w, rows, layer, pieces, tm, tn, name):
    m, d = x.shape
    n = w.shape[1]
    return pl.pallas_call(
        _nmm_kernel,
        out_shape=jax.ShapeDtypeStruct((m, n), F32),
        grid=(m // tm, n // tn),
        in_specs=[pl.BlockSpec((tm, d), lambda i, j: (i, 0)),
                  pl.BlockSpec((1, d), lambda i, j: (0, 0)),
                  _mod_spec(rows, tm, layer, pieces[0], d),
                  _mod_spec(rows, tm, layer, pieces[1], d),
                  pl.BlockSpec((d, tn), lambda i, j: (0, j))],
        out_specs=pl.BlockSpec((tm, tn), lambda i, j: (i, j)),
        scratch_shapes=[pltpu.VMEM((tm, d), BF16)],
        compiler_params=_cp("parallel", "arbitrary"),
        name=name,
    )(x, g, mod5, mod5, w)


def _out_kernel(mix_ref, w_ref, x_ref, g_ref, gate_ref, o_ref):
    y = _dot(mix_ref[...], w_ref[...])
    o_ref[...] = x_ref[...] + gate_ref[...] * (_rms(y) * g_ref[...])


def _out_proj(mix, w, x, g, mod5, rows, layer, tm, name):
    m, d = x.shape
    k = mix.shape[1]
    return pl.pallas_call(
        _out_kernel,
        out_shape=jax.ShapeDtypeStruct((m, d), F32),
        grid=(m // tm,),
        in_specs=[pl.BlockSpec((tm, k), lambda i: (i, 0)),
                  pl.BlockSpec((k, d), lambda i: (0, 0)),
                  pl.BlockSpec((tm, d), lambda i: (i, 0)),
                  pl.BlockSpec((1, d), lambda i: (0, 0)),
                  _mod_spec(rows, tm, layer, 2, d)],
        out_specs=pl.BlockSpec((tm, d), lambda i: (i, 0)),
        compiler_params=_cp("parallel"),
        name=name,
    )(mix, w, x, g, mod5)


def _ffn_kernel(x_ref, g2_ref, sh_ref, sc_ref, wg_ref, wu_ref, wd_ref, g3_ref, gate_ref, o_ref, h_ref, acc_ref):
    j = pl.program_id(1)

    @pl.when(j == 0)
    def _():
        h = _rms(x_ref[...]) * g2_ref[...]
        h_ref[...] = (h * (1.0 + sc_ref[...]) + sh_ref[...]).astype(BF16)
        acc_ref[...] = jnp.zeros(acc_ref.shape, F32)

    h = h_ref[...]
    a = _dot(h, wg_ref[...])
    t = a * _sigmoid(a) * _dot(h, wu_ref[...])
    acc_ref[...] += _dot(t.astype(BF16), wd_ref[...])

    @pl.when(j == pl.num_programs(1) - 1)
    def _():
        o_ref[...] = x_ref[...] + gate_ref[...] * (_rms(acc_ref[...]) * g3_ref[...])


def _ffn(x, g2, g3, mod5, wg, wu, wd, rows, layer, tm, tf):
    m, d = x.shape
    f = wg.shape[1]
    return pl.pallas_call(
        _ffn_kernel,
        out_shape=jax.ShapeDtypeStruct((m, d), F32),
        grid=(m // tm, f // tf),
        in_specs=[pl.BlockSpec((tm, d), lambda i, j: (i, 0)),
                  pl.BlockSpec((1, d), lambda i, j: (0, 0)),
                  _mod_spec(rows, tm, layer, 3, d),
                  _mod_spec(rows, tm, layer, 4, d),
                  pl.BlockSpec((d, tf), lambda i, j: (0, j)),
                  pl.BlockSpec((d, tf), lambda i, j: (0, j)),
                  pl.BlockSpec((tf, d), lambda i, j: (j, 0)),
                  pl.BlockSpec((1, d), lambda i, j: (0, 0)),
                  _mod_spec(rows, tm, layer, 5, d)],
        out_specs=pl.BlockSpec((tm, d), lambda i, j: (i, 0)),
        scratch_shapes=[pltpu.VMEM((tm, d), BF16), pltpu.VMEM((tm, d), F32)],
        compiler_params=_cp("parallel", "arbitrary"),
        name=f"ffn{layer}",
    )(x, g2, mod5, mod5, wg, wu, wd, g3, mod5)


def _ret_kernel(ld_ref, qf_ref, kf_ref, vf_ref, qb_ref, kb_ref, vb_ref, s0_ref,
                of_ref, ob_ref, sfin_ref, s_scr, mask_scr, xi_scr, zeta_scr):
    c = RET_CHUNK
    b, i = pl.program_id(0), pl.program_id(1)

    @pl.when((b == 0) & (i == 0))
    def _():
        row = lax.broadcasted_iota(jnp.int32, (c, c), 0).astype(F32)
        col = lax.broadcasted_iota(jnp.int32, (c, c), 1).astype(F32)
        for d in range(2):
            diff = row - col if d == 0 else col - row
            pos = row if d == 0 else (c - 1.0) - row
            for h in range(RET_HEADS):
                lg = -jnp.exp(ld_ref[d, h])
                mask_scr[d, h] = jnp.where(diff >= 0, jnp.exp(lg * jnp.maximum(diff, 0.0)), 0.0)
                xi_scr[d, h] = jnp.exp(lg * (pos + 1.0))
                zeta_scr[d, h] = jnp.exp(lg * ((c - 1.0) - pos))

    @pl.when(i == 0)
    def _():
        s_scr[...] = s0_ref[...]

    for d, (q_ref, k_ref, v_ref, o_ref) in enumerate(((qf_ref, kf_ref, vf_ref, of_ref),
                                                      (qb_ref, kb_ref, vb_ref, ob_ref))):
        for h in range(RET_HEADS):
            q = q_ref[:, h * RET_DK:(h + 1) * RET_DK]
            k = k_ref[:, h * RET_DK:(h + 1) * RET_DK] * (RET_DK ** -0.5)
            v = v_ref[:, h * RET_DV:(h + 1) * RET_DV].astype(BF16)
            s = s_scr[d, h]
            qb = q.astype(BF16)
            scores = _dot_nt(qb, k.astype(BF16)) * mask_scr[d, h]
            inner = _dot(scores.astype(BF16), v)
            cross = _dot(qb, s.astype(BF16)) * xi_scr[d, h]
            o_ref[:, h * RET_DV:(h + 1) * RET_DV] = inner + cross
            kz = (k * zeta_scr[d, h][:, :RET_DK]).astype(BF16)
            gamma_c = jnp.exp(-jnp.exp(ld_ref[d, h]) * float(c))
            s_scr[d, h] = gamma_c * s + _dot_tn(kz, v)

    @pl.when(i == pl.num_programs(1) - 1)
    def _():
        sfin_ref[...] = s_scr[...]


def _retention(pa, ld_b, s0, row_off, n_seq, seq_len):
    c = RET_CHUNK
    nc = seq_len // c
    off = row_off // c
    a_qk, a_v = RET_HEADS * RET_DK, RET_HEADS * RET_DV

    def fwd(col):
        return lambda b, i: (off + b * nc + i, col)

    def bwd(col):
        return lambda b, i: (off + b * nc + (nc - 1 - i), col)

    st_spec = pl.BlockSpec((None, 2, RET_HEADS, RET_DK, RET_DV), lambda b, i: (b, 0, 0, 0, 0))
    return pl.pallas_call(
        _ret_kernel,
        out_shape=(jax.ShapeDtypeStruct((n_seq * seq_len, a_v), F32),
                   jax.ShapeDtypeStruct((n_seq * seq_len, a_v), F32),
                   jax.ShapeDtypeStruct((n_seq, 2, RET_HEADS, RET_DK, RET_DV), F32)),
        grid=(n_seq, nc),
        in_specs=[pl.BlockSpec((2, RET_HEADS, 1, c), lambda b, i: (0, 0, 0, 0)),
                  pl.BlockSpec((c, a_qk), fwd(0)), pl.BlockSpec((c, a_qk), fwd(1)), pl.BlockSpec((c, a_v), fwd(1)),
                  pl.BlockSpec((c, a_qk), bwd(0)), pl.BlockSpec((c, a_qk), bwd(1)), pl.BlockSpec((c, a_v), bwd(1)),
                  st_spec],
        out_specs=(pl.BlockSpec((c, a_v), lambda b, i: (b * nc + i, 0)),
                   pl.BlockSpec((c, a_v), lambda b, i: (b * nc + (nc - 1 - i), 0)),
                   st_spec),
        scratch_shapes=[pltpu.VMEM((2, RET_HEADS, RET_DK, RET_DV), F32),
                        pltpu.VMEM((2, RET_HEADS, c, c), F32),
                        pltpu.VMEM((2, RET_HEADS, c, c), F32),
                        pltpu.VMEM((2, RET_HEADS, c, c), F32)],
        compiler_params=_cp("arbitrary", "arbitrary"),
        name="retention",
    )(ld_b, pa, pa, pa, pa, pa, pa, s0)


def _pair_sum(x):
    r = lax.broadcasted_iota(jnp.int32, (128, 128), 0) // RWKV_N
    c = lax.broadcasted_iota(jnp.int32, (128, 128), 1) // RWKV_N
    return _dot(x, (r == c).astype(F32), HI)


def _prep_kernel(rows, tr, r_ref, k_ref, v_ref, s_ref, rp_ref, kp_ref, vp_ref, sp_ref,
                 rn_ref, kn_ref, vn_ref, sn_ref, cw_ref, w0_ref, wup_ref, a0_ref, aup_ref, gup_ref,
                 kk_ref, ka_ref, rk_ref,
                 r_o, v_o, kkn_o, gb_o, bonus_o, lw_o, kd_o, b_o):
    r0 = pl.program_id(0) * tr
    in_p = r0 < rows.rows_p
    pos = jnp.where(in_p, r0 % rows.l_p, (r0 - rows.rows_p) % rows.l_s)
    seq_len = jnp.where(in_p, rows.l_p, rows.l_s)
    has_prev = (pos != 0).astype(F32)
    has_next = (pos + tr != seq_len).astype(F32)
    rid = lax.broadcasted_iota(jnp.int32, (tr, 1), 0)

    def conv(cur_ref, prev_ref, next_ref, lo, hi):
        cur = cur_ref[...]
        w = cw_ref[:, lo:hi]
        up = jnp.where(rid == 0, prev_ref[7:8, :] * has_prev, pltpu.roll(cur, 1, 0))
        dn = jnp.where(rid == tr - 1, next_ref[0:1, :] * has_next, pltpu.roll(cur, tr - 1, 0))
        return up * w[0:1] + cur * w[1:2] + dn * w[2:3]

    w = RWKV_W
    r = conv(r_ref, rp_ref, rn_ref, 0, w)
    kb = conv(k_ref, kp_ref, kn_ref, w, 2 * w)
    vb = conv(v_ref, vp_ref, vn_ref, 2 * w, 3 * w)
    sm = conv(s_ref, sp_ref, sn_ref, 3 * w, 3 * w + 384)
    gc, wc, ac = sm[:, 0:128], sm[:, 128:256], sm[:, 256:384]

    gb = _dot(_sigmoid(gc), gup_ref[...], HI)
    twc = jnp.tanh(wc)
    lws, ads = [], []
    for d in range(2):
        lws.append(-DECAY_SCALE * _sigmoid(w0_ref[d:d + 1, :] + _dot(twc, wup_ref[d], HI)))
        ads.append(_sigmoid(a0_ref[d:d + 1, :] + _dot(ac, aup_ref[d], HI)))

    for p in range(RWKV_PAIRS):
        sl = slice(p * 128, (p + 1) * 128)
        rp, kp, vp = r[:, sl], kb[:, sl], vb[:, sl]
        kkf = kp * kk_ref[:, sl]
        kkn = kkf / jnp.maximum(jnp.sqrt(_pair_sum(kkf * kkf)), L2_EPS)
        r_o[p] = rp
        v_o[p] = vp
        kkn_o[p] = kkn
        gb_o[p] = gb[:, sl]
        bonus = jnp.zeros_like(rp)
        for d in range(2):
            ad = ads[d][:, sl]
            kd = kp * (1.0 + (ad - 1.0) * ka_ref[:, sl])
            lw_o[d, p] = lws[d][:, sl]
            kd_o[d, p] = kd
            b_o[d, p] = kkn * ad
            bonus = bonus + _pair_sum(rp * kd * rk_ref[:, sl]) * vp
        bonus_o[p] = bonus


def _rwkv_prep(pb, rows, cw, w0, wup, a0, aup, gup, kk, ka, rk, tr):
    m = pb.shape[0]
    w = RWKV_W
    nb8 = m // 8

    def cur(width, col):
        return pl.BlockSpec((tr, width), lambda i: (i, col))

    def prev(width, col):
        return pl.BlockSpec((8, width), lambda i: (jnp.maximum(i * (tr // 8) - 1, 0), col))

    def nxt(width, col):
        return pl.BlockSpec((8, width), lambda i: (jnp.minimum((i + 1) * (tr // 8), nb8 - 1), col))

    def full(a):
        nd = a.ndim
        return pl.BlockSpec(a.shape, lambda i: (0,) * nd)

    pm = jax.ShapeDtypeStruct((RWKV_PAIRS, m, 128), F32)
    pm2 = jax.ShapeDtypeStruct((2, RWKV_PAIRS, m, 128), F32)
    o1 = pl.BlockSpec((RWKV_PAIRS, tr, 128), lambda i: (0, i, 0))
    o2 = pl.BlockSpec((2, RWKV_PAIRS, tr, 128), lambda i: (0, 0, i, 0))
    small_col = (3 * w) // 384
    params = (cw, w0, wup, a0, aup, gup, kk, ka, rk)
    return pl.pallas_call(
        functools.partial(_prep_kernel, rows, tr),
        out_shape=(pm, pm, pm, pm, pm, pm2, pm2, pm2),
        grid=(m // tr,),
        in_specs=[cur(w, 0), cur(w, 1), cur(w, 2), cur(384, small_col),
                  prev(w, 0), prev(w, 1), prev(w, 2), prev(384, small_col),
                  nxt(w, 0), nxt(w, 1), nxt(w, 2), nxt(384, small_col)] + [full(a) for a in params],
        out_specs=(o1, o1, o1, o1, o1, o2, o2, o2),
        compiler_params=_cp("parallel"),
        name="rwkv_prep",
    )(pb, pb, pb, pb, pb, pb, pb, pb, pb, pb, pb, pb, *params)


def _split(x):
    hi = x.astype(BF16)
    return hi, (x - hi.astype(F32)).astype(BF16)


def _mm(a, b, dims, passes):
    def dg(x, y):
        return lax.dot_general(x, y, (dims, ((), ())), preferred_element_type=F32)

    if passes == 1:
        return dg(a.astype(BF16), b.astype(BF16))
    ah, al = _split(a)
    bh, bl = _split(b)
    return dg(ah, bh) + (dg(ah, bl) + dg(al, bh))


_NN = ((1,), (0,))
_NT = ((1,), (1,))
_TN = ((0,), (0,))


def _bd(x):
    first = lax.broadcasted_iota(jnp.int32, x.shape, 1) < RWKV_N
    return jnp.concatenate([jnp.where(first, x, 0.0), jnp.where(first, 0.0, x)], axis=0)


def _scan_pair(d, passes, lw, r, v, kk, kd, b, t_bd):
    c, n = RWKV_CHUNK, RWKV_N
    mm = functools.partial(_mm, passes=passes)
    row = lax.broadcasted_iota(jnp.int32, (c, c), 0)
    col = lax.broadcasted_iota(jnp.int32, (c, c), 1)
    tri = (row >= col) if d == 0 else (row <= col)
    last = c - 1 if d == 0 else 0
    lw_hi, lw_lo = _split(lw)
    tri_b = tri.astype(BF16)
    cum = _dot(tri_b, lw_hi) + _dot(tri_b, lw_lo)
    yield
    tot = cum[last:last + 1, :]
    e_neg = jnp.exp(-cum)
    e_end = jnp.exp(tot - cum)
    kkt = kk * jnp.exp(cum - lw)
    rt = r * jnp.exp(cum)
    kh, bh = kd * e_neg, b * e_neg
    kbar, bbar = kd * e_end, b * e_end

    t4 = lax.broadcasted_iota(jnp.int32, (2 * c, 4 * n), 0)
    s4 = lax.broadcasted_iota(jnp.int32, (2 * c, 4 * n), 1) % n
    before = (s4 < t4 % c) if d == 0 else (s4 > t4 % c)
    keep = before | ((t4 >= c) & (s4 == t4 % c))
    a4 = mm(jnp.concatenate([kkt, rt], axis=0), jnp.concatenate([_bd(bh), _bd(kh)], axis=0), _NT)
    kv = mm(kbar, v, _TN)
    yield
    a4 = jnp.where(keep, a4, 0.0)
    a_b, a_rb = a4[:c, :2 * n], a4[c:, :2 * n]
    av = mm(a4[:, 2 * n:], _bd(v), _NN)
    npow = mm(-a_b, _bd(-a_b), _NN)
    yield
    akv, arkv = av[:c], av[c:]
    eye = (lax.broadcasted_iota(jnp.int32, (c, 2 * n), 0)
           == lax.broadcasted_iota(jnp.int32, (c, 2 * n), 1) % n).astype(F32)
    minv = eye - a_b
    for _ in range(4):
        both = mm(jnp.concatenate([npow, minv], axis=0), _bd(npow), _NN)
        yield
        npow, minv = both[:c], minv + both[c:]
    last_term = mm(minv, _bd(npow), _NN)
    yield
    minv = minv + last_term

    w12 = mm(minv, jnp.concatenate([_bd(kkt), _bd(akv)], axis=1), _NN)
    yield
    w1, w2 = w12[:, :2 * n], w12[:, 2 * n:]
    aw = mm(a_rb, jnp.concatenate([_bd(w1), _bd(w2)], axis=1), _NN)
    gh = mm(bbar, w12, _TN)
    yield
    qy = jnp.concatenate([rt, arkv], axis=1) - aw
    q1, y0 = qy[:, :2 * n], qy[:, 2 * n:]
    r2 = lax.broadcasted_iota(jnp.int32, (2 * n, 2 * n), 0)
    c2 = lax.broadcasted_iota(jnp.int32, (2 * n, 2 * n), 1)
    same_head = (r2 // n) == (c2 // n)
    g_bd = jnp.where(same_head, jnp.where(r2 == c2, jnp.exp(tot), 0.0) - gh[:, :2 * n], 0.0)
    h_bd = jnp.where(same_head, kv - gh[:, 2 * n:], 0.0)
    yt = mm(jnp.concatenate([q1, g_bd], axis=0), t_bd, _NN)
    yield
    return yt[:c] + y0, yt[c:] + h_bd


def _round_robin(gens):
    outs = [None] * len(gens)
    live = list(range(len(gens)))
    while live:
        for k in list(live):
            try:
                next(gens[k])
            except StopIteration as stop:
                outs[k] = stop.value
                live.remove(k)
    return outs


def _scan_kernel(passes, unroll, rf, vf, kkf, lwf, kdf, bf, rb, vb, kkb, lwb, kdb, bb, s0_ref,
                 yf_ref, yb_ref, sfin_ref, t_scr):
    i = pl.program_id(1)
    n = RWKV_N

    @pl.when(i == 0)
    def _():
        z = jnp.zeros((n, n), F32)
        for d in range(2):
            for p in range(RWKV_PAIRS):
                ta, tb = s0_ref[d, 2 * p].T, s0_ref[d, 2 * p + 1].T
                t_scr[d, p] = jnp.concatenate([jnp.concatenate([ta, z], axis=1),
                                               jnp.concatenate([z, tb], axis=1)], axis=0)

    dirs = ((rf, vf, kkf, lwf, kdf, bf, yf_ref), (rb, vb, kkb, lwb, kdb, bb, yb_ref))

    def group(q, carry):
        chains = [(d, q * unroll + u) for u in range(unroll) for d in range(2)]
        args = []
        for d, p in chains:
            r_ref, v_ref, kk_ref, lw_ref, kd_ref, b_ref, _ = dirs[d]
            args.append((lw_ref[p], r_ref[p], v_ref[p], kk_ref[p], kd_ref[p], b_ref[p], t_scr[d, p]))
        outs = _round_robin([_scan_pair(d, passes, *a) for (d, _), a in zip(chains, args)])
        for (d, p), (y, t1) in zip(chains, outs):
            t_scr[d, p] = t1
            dirs[d][6][p] = y
        return carry

    lax.fori_loop(0, RWKV_PAIRS // unroll, group, 0)

    @pl.when(i == pl.num_programs(1) - 1)
    def _():
        for d in range(2):
            for p in range(RWKV_PAIRS):
                t = t_scr[d, p]
                sfin_ref[d, 2 * p] = t[:n, :n].T
                sfin_ref[d, 2 * p + 1] = t[n:, n:].T


def _rwkv_scan(r, v, kk, lw, kd, b, s0, row_off, n_seq, seq_len, passes, unroll):
    c = RWKV_CHUNK
    nc = seq_len // c
    off = row_off // c

    def fwd3(b_, i):
        return (0, off + b_ * nc + i, 0)

    def bwd3(b_, i):
        return (0, off + b_ * nc + (nc - 1 - i), 0)

    def one(index):
        return pl.BlockSpec((RWKV_PAIRS, c, 128), index)

    def per_dir(d, index):
        return pl.BlockSpec((None, RWKV_PAIRS, c, 128), lambda b_, i: (d,) + index(b_, i))

    st_spec = pl.BlockSpec((None, 2, RWKV_HEADS, RWKV_N, RWKV_N), lambda b_, i: (b_, 0, 0, 0, 0))
    ysh = jax.ShapeDtypeStruct((RWKV_PAIRS, n_seq * seq_len, 128), F32)
    return pl.pallas_call(
        functools.partial(_scan_kernel, passes, unroll),
        out_shape=(ysh, ysh, jax.ShapeDtypeStruct((n_seq, 2, RWKV_HEADS, RWKV_N, RWKV_N), F32)),
        grid=(n_seq, nc),
        in_specs=[one(fwd3), one(fwd3), one(fwd3), per_dir(0, fwd3), per_dir(0, fwd3), per_dir(0, fwd3),
                  one(bwd3), one(bwd3), one(bwd3), per_dir(1, bwd3), per_dir(1, bwd3), per_dir(1, bwd3),
                  st_spec],
        out_specs=(pl.BlockSpec((RWKV_PAIRS, c, 128), lambda b_, i: (0, b_ * nc + i, 0)),
                   pl.BlockSpec((RWKV_PAIRS, c, 128), lambda b_, i: (0, b_ * nc + (nc - 1 - i), 0)),
                   st_spec),
        scratch_shapes=[pltpu.VMEM((2, RWKV_PAIRS, 2 * RWKV_N, 2 * RWKV_N), F32)],
        compiler_params=_cp("arbitrary", "arbitrary"),
        name="rwkv_scan",
    )(r, v, kk, lw, kd, b, r, v, kk, lw, kd, b, s0)


def _mix0_kernel(of_ref, ob_ref, g_ref, yf_ref, yb_ref, bonus_ref, gb_ref, lnw_ref, lnb_ref, o_ref):
    for h in range(RET_HEADS):
        sl = slice(h * RET_DV, (h + 1) * RET_DV)
        g = g_ref[:, sl]
        o_ref[:, sl] = (_rms(of_ref[:, sl] + ob_ref[:, sl]) * (g * _sigmoid(g))).astype(o_ref.dtype)
    base = RET_HEADS * RET_DV
    for p in range(RWKV_PAIRS):
        sl = slice(p * 128, (p + 1) * 128)
        y = yf_ref[p] + yb_ref[p]
        mu = _pair_sum(y) * (1.0 / RWKV_N)
        yc = y - mu
        var = _pair_sum(yc * yc) * (1.0 / RWKV_N)
        yn = yc * lax.rsqrt(var + RWKV_LN_EPS) * lnw_ref[:, sl] + lnb_ref[:, sl] + bonus_ref[p]
        o_ref[:, base + p * 128:base + (p + 1) * 128] = (yn * gb_ref[p]).astype(o_ref.dtype)


def _mix0(o_f, o_b, pa, y_f, y_b, bonus, gb, lnw, lnb, tr):
    m = o_f.shape[0]
    a_v = RET_HEADS * RET_DV
    row = pl.BlockSpec((tr, a_v), lambda i: (i, 0))
    pm = pl.BlockSpec((RWKV_PAIRS, tr, 128), lambda i: (0, i, 0))
    vec = pl.BlockSpec((1, RWKV_W), lambda i: (0, 0))
    return pl.pallas_call(
        _mix0_kernel,
        out_shape=jax.ShapeDtypeStruct((m, a_v + RWKV_W), BF16),
        grid=(m // tr,),
        in_specs=[row, row, pl.BlockSpec((tr, a_v), lambda i: (i, 2)), pm, pm, pm, pm, vec, vec],
        out_specs=pl.BlockSpec((tr, a_v + RWKV_W), lambda i: (i, 0)),
        compiler_params=_cp("parallel"),
        name="mix0",
    )(o_f, o_b, pa, y_f, y_b, bonus, gb, lnw, lnb)


def _rot_half(x):
    lane = lax.broadcasted_iota(jnp.int32, x.shape, 1) % 64
    return jnp.where(lane < 32, pltpu.roll(x, 96, 1), pltpu.roll(x, 32, 1))


def _qk_kernel(use_rope, qkv_ref, qg_ref, kg_ref, cos_ref, sin_ref, q_o, k_o, kn_o, v_o):
    hd = ATT_HD
    if use_rope:
        cos, sin = cos_ref[...], sin_ref[...]
    for h in range(ATT_HEADS):
        q = _rms(qkv_ref[:, h * hd:(h + 1) * hd]) * qg_ref[...]
        if use_rope:
            q = q * cos + _rot_half(q) * sin
        q_o[:, h * hd:(h + 1) * hd] = (q * (hd ** -0.5 * LOG2E)).astype(q_o.dtype)
    kbase = ATT_HEADS * hd
    vbase = kbase + ATT_KV_HEADS * hd
    for h in range(ATT_KV_HEADS):
        k = _rms(qkv_ref[:, kbase + h * hd:kbase + (h + 1) * hd]) * kg_ref[...]
        kn_o[:, h * hd:(h + 1) * hd] = k
        if use_rope:
            k = k * cos + _rot_half(k) * sin
        k_o[:, h * hd:(h + 1) * hd] = k.astype(k_o.dtype)
    v_o[...] = qkv_ref[:, vbase:vbase + ATT_KV_HEADS * hd].astype(v_o.dtype)


def _qk_post(qkv, qg, kg, cos, sin, row_off, n_rows, seq_len, use_rope, tr):
    c_mix, kv_w = ATT_HEADS * ATT_HD, ATT_KV_HEADS * ATT_HD
    off = row_off // tr
    per_seq = seq_len // tr
    tab = pl.BlockSpec((tr, ATT_HD), lambda i: (i % per_seq, 0))
    vec = pl.BlockSpec((1, ATT_HD), lambda i: (0, 0))
    return pl.pallas_call(
        functools.partial(_qk_kernel, use_rope),
        out_shape=(jax.ShapeDtypeStruct((n_rows, c_mix), BF16),
                   jax.ShapeDtypeStruct((n_rows, kv_w), BF16),
                   jax.ShapeDtypeStruct((n_rows, kv_w), F32),
                   jax.ShapeDtypeStruct((n_rows, kv_w), BF16)),
        grid=(n_rows // tr,),
        in_specs=[pl.BlockSpec((tr, c_mix + 2 * kv_w), lambda i: (off + i, 0)), vec, vec, tab, tab],
        out_specs=(pl.BlockSpec((tr, c_mix), lambda i: (i, 0)),
                   pl.BlockSpec((tr, kv_w), lambda i: (i, 0)),
                   pl.BlockSpec((tr, kv_w), lambda i: (i, 0)),
                   pl.BlockSpec((tr, kv_w), lambda i: (i, 0))),
        compiler_params=_cp("parallel"),
        name="qk_post",
    )(qkv, qg, kg, cos, sin)


ATT_VROWS = ATT_HD + 8


def _attn_kernel(q_ref, k_ref, vt_ref, o_ref, m_scr, acc_scr):
    j = pl.program_id(3)
    hd = ATT_HD

    @pl.when(j == 0)
    def _():
        m_scr[...] = jnp.full(m_scr.shape, -jnp.inf, F32)
        acc_scr[...] = jnp.zeros(acc_scr.shape, F32)

    k = k_ref[...]
    vt = vt_ref[...]

    def scores(g):
        return _dot_nt(k, q_ref[:, g * hd:(g + 1) * hd])

    st = scores(0)
    for g in range(ATT_GROUP):
        st_next = scores(g + 1) if g + 1 < ATT_GROUP else None
        m_old = m_scr[g]
        m_new = jnp.maximum(m_old, jnp.max(st, axis=0, keepdims=True))
        p = jnp.exp2(st - m_new).astype(BF16)
        acc_scr[g] = jnp.exp2(m_old - m_new) * acc_scr[g] + _dot(vt, p)
        m_scr[g] = m_new
        st = st_next

    @pl.when(j == pl.num_programs(3) - 1)
    def _():
        for g in range(ATT_GROUP):
            acc = acc_scr[g]
            o_ref[:, g * hd:(g + 1) * hd] = (acc[:hd] / acc[hd:hd + 1]).T.astype(o_ref.dtype)


def _attention(q, k_all, vt_all, n_seq, lq, tq, tk):
    lk = k_all.shape[1]
    nq = lq // tq
    gw = ATT_GROUP * ATT_HD
    return pl.pallas_call(
        _attn_kernel,
        out_shape=jax.ShapeDtypeStruct(q.shape, BF16),
        grid=(n_seq, ATT_KV_HEADS, nq, lk // tk),
        in_specs=[pl.BlockSpec((tq, gw), lambda b, h, i, j: (b * nq + i, h)),
                  pl.BlockSpec((None, tk, ATT_HD), lambda b, h, i, j: (b, j, h)),
                  pl.BlockSpec((None, None, ATT_VROWS, tk), lambda b, h, i, j: (b, h, 0, j))],
        out_specs=pl.BlockSpec((tq, gw), lambda b, h, i, j: (b * nq + i, h)),
        scratch_shapes=[pltpu.VMEM((ATT_GROUP, 1, tq), F32),
                        pltpu.VMEM((ATT_GROUP, ATT_VROWS, tq), F32)],
        compiler_params=_cp("parallel", "parallel", "parallel", "arbitrary"),
        name="attention",
    )(q, k_all, vt_all)


def _v_transposed(v):
    n, lk, _ = v.shape
    vt = jnp.swapaxes(v.reshape(n, lk, ATT_KV_HEADS, ATT_HD), 1, 2)
    vt = jnp.swapaxes(vt, 2, 3)
    extra = jnp.zeros((n, ATT_KV_HEADS, ATT_VROWS - ATT_HD, lk), BF16).at[:, :, 0].set(1.0)
    return jnp.concatenate([vt, extra], axis=2)


def _rope_tables(seq_len):
    t = jnp.arange(seq_len, dtype=jnp.int32)
    row = (t // GRID_W).astype(F32)
    col = (t % GRID_W).astype(F32)
    axis_dim = ATT_HD // 2
    freqs = jnp.power(ROPE_THETA, -jnp.arange(0, axis_dim, 2, dtype=F32) / axis_dim)
    ar, ac = row[:, None] * freqs[None, :], col[:, None] * freqs[None, :]
    cos = jnp.concatenate([jnp.cos(ar), jnp.cos(ar), jnp.cos(ac), jnp.cos(ac)], axis=-1)
    sin = jnp.concatenate([-jnp.sin(ar), jnp.sin(ar), -jnp.sin(ac), jnp.sin(ac)], axis=-1)
    return cos, sin


def _pad_cols(w, n):
    return jnp.pad(w, ((0, 0), (0, n - w.shape[1])))


def _pad_rank(w_up):
    z = jnp.zeros_like(w_up[0])
    return jnp.stack([jnp.concatenate([w_up[0], z], axis=0), jnp.concatenate([z, w_up[1]], axis=0)])


def kernel(x_prompt, x_sample, state_ret_fwd, state_ret_bwd, state_rwkv_fwd, state_rwkv_bwd, cache_k, cache_v,
           c, c_ctx, ada_w, ada_b, norm_g, ffn_w_gate, ffn_w_up, ffn_w_down, ab_w_in, ab_w_out, ret_log_decay,
           rwkv_conv_w, rwkv_w0, rwkv_w_up, rwkv_a0, rwkv_a_up, rwkv_g_up, rwkv_k_k, rwkv_k_a, rwkv_r_k,
           rwkv_ln_w, rwkv_ln_b, c_w_in, c_w_out, c_q_norm, c_k_norm):
    n_p, l_p, d = x_prompt.shape
    n_s, l_s, _ = x_sample.shape
    rows = _Rows(n_p, l_p, n_s, l_s)
    rp = rows.rows_p
    a_in = 2 * RET_HEADS * RET_DK + 2 * RET_HEADS * RET_DV
    w = RWKV_W

    x = jnp.concatenate([x_prompt.reshape(rp, d), x_sample.reshape(n_s * l_s, d)], axis=0)
    cond8 = jnp.zeros((8, d), F32).at[:n_s].set(c).at[n_s].set(c_ctx)
    mod5 = _ada(cond8, ada_w, ada_b).reshape(ada_w.shape[0], 8, 6, 1, d)

    tm = 512
    g0 = norm_g[0]
    w_a = ab_w_in[0][:, :a_in].astype(BF16)
    w_b = _pad_cols(ab_w_in[0][:, a_in:], 3584).astype(BF16)
    pa = _norm_mod_matmul(x, g0[0:1], mod5, w_a, rows, 0, (0, 1), tm, 512, "proj_ret")
    pb = _norm_mod_matmul(x, g0[0:1], mod5, w_b, rows, 0, (0, 1), tm, 512, "proj_rwkv")

    ld_b = jnp.broadcast_to(ret_log_decay[0][:, :, None, None], (2, RET_HEADS, 1, RET_CHUNK))
    zr = jnp.zeros((n_p, 2, RET_HEADS, RET_DK, RET_DV), F32)
    sr = jnp.stack([state_ret_fwd[:, 0], state_ret_bwd[:, 0]], axis=1)
    of_p, ob_p, ret_fin = _retention(pa, ld_b, zr, 0, n_p, l_p)
    of_s, ob_s, _ = _retention(pa, ld_b, sr, rp, n_s, l_s)

    cw = _pad_cols(rwkv_conv_w[0], 3584)
    r_, v_, kk_, gb_, bonus_, lw_, kd_, b_ = _rwkv_prep(
        pb, rows, cw, rwkv_w0[0], _pad_rank(rwkv_w_up[0]), rwkv_a0[0], _pad_rank(rwkv_a_up[0]), rwkv_g_up[0],
        rwkv_k_k[0][None], rwkv_k_a[0][None], rwkv_r_k[0].reshape(1, w), 256)
    zw = jnp.zeros((n_p, 2, RWKV_HEADS, RWKV_N, RWKV_N), F32)
    sw = jnp.stack([state_rwkv_fwd[:, 0], state_rwkv_bwd[:, 0]], axis=1)
    yf_p, yb_p, rwkv_fin = _rwkv_scan(r_, v_, kk_, lw_, kd_, b_, zw, 0, n_p, l_p, SCAN_PASSES_P, 4)
    yf_s, yb_s, _ = _rwkv_scan(r_, v_, kk_, lw_, kd_, b_, sw, rp, n_s, l_s, SCAN_PASSES_S, 4)

    o_f = jnp.concatenate([of_p, of_s], axis=0)
    o_b = jnp.concatenate([ob_p, ob_s], axis=0)
    y_f = jnp.concatenate([yf_p, yf_s], axis=1)
    y_b = jnp.concatenate([yb_p, yb_s], axis=1)
    mix = _mix0(o_f, o_b, pa, y_f, y_b, bonus_, gb_, rwkv_ln_w[0][None], rwkv_ln_b[0][None], 256)
    x = _out_proj(mix, ab_w_out[0].astype(BF16), x, g0[1:2], mod5, rows, 0, tm, "out_proj0")
    x = _ffn(x, g0[2:3], g0[3:4], mod5, ffn_w_gate[0].astype(BF16), ffn_w_up[0].astype(BF16),
             ffn_w_down[0].astype(BF16), rows, 0, tm, 512)

    g1 = norm_g[1]
    kv_w = ATT_KV_HEADS * ATT_HD
    qkv = _norm_mod_matmul(x, g1[0:1], mod5, c_w_in[0].astype(BF16), rows, 1, (0, 1), tm, 512, "proj_qkv")
    cos, sin = _rope_tables(l_s)
    qg, kg = c_q_norm[0][None], c_k_norm[0][None]
    q_p, k_p, kn_p, v_p = _qk_post(qkv, qg, kg, cos, sin, 0, rp, l_p, False, 256)
    q_s, k_s, _, v_s = _qk_post(qkv, qg, kg, cos, sin, rp, n_s * l_s, l_s, True, 256)
    o_p = _attention(q_p, k_p.reshape(n_p, l_p, kv_w), _v_transposed(v_p.reshape(n_p, l_p, kv_w)), n_p, l_p, l_p, l_p)
    past = cache_k.shape[2]
    k_all = jnp.concatenate([cache_k[:, 0].reshape(n_s, past, kv_w).astype(BF16), k_s.reshape(n_s, l_s, kv_w)], axis=1)
    v_all = jnp.concatenate([cache_v[:, 0].reshape(n_s, past, kv_w).astype(BF16), v_s.reshape(n_s, l_s, kv_w)], axis=1)
    o_s = _attention(q_s, k_all, _v_transposed(v_all), n_s, l_s, 512, 512)
    o = jnp.concatenate([o_p, o_s], axis=0)
    x = _out_proj(o, c_w_out[0].astype(BF16), x, g1[1:2], mod5, rows, 1, tm, "out_proj1")
    x = _ffn(x, g1[2:3], g1[3:4], mod5, ffn_w_gate[1].astype(BF16), ffn_w_up[1].astype(BF16),
             ffn_w_down[1].astype(BF16), rows, 1, tm, 512)

    y_prompt = x[:rp].reshape(n_p, l_p, d)
    y_sample = x[rp:].reshape(n_s, l_s, d)
    new_k = kn_p.reshape(n_p, 1, l_p, ATT_KV_HEADS, ATT_HD)
    new_v = qkv[:rp, ATT_HEADS * ATT_HD + kv_w:].reshape(n_p, 1, l_p, ATT_KV_HEADS, ATT_HD)
    return (y_prompt, y_sample, ret_fin[:, 0][:, None], ret_fin[:, 1][:, None],
            rwkv_fin[:, 0][:, None], rwkv_fin[:, 1][:, None], new_k, new_v)
```

```python
import functools

import jax
import jax.numpy as jnp
from jax import lax
from jax.experimental import pallas as pl
from jax.experimental.pallas import tpu as pltpu

F32 = jnp.float32
BF16 = jnp.bfloat16
HI = lax.Precision.HIGHEST

RMS_EPS = 1e-6
RWKV_LN_EPS = 64e-5
L2_EPS = 1e-12
DECAY_SCALE = 0.606531
ROPE_THETA = 10000.0
GRID_W = 64
LOG2E = 1.4426950408889634

RET_HEADS, RET_DK, RET_DV, RET_CHUNK = 8, 64, 128, 128
RWKV_HEADS, RWKV_N = 16, 64
RWKV_PAIRS = RWKV_HEADS // 2
RWKV_W = RWKV_HEADS * RWKV_N
RWKV_CHUNK = 64
SCAN_PASSES_P, SCAN_PASSES_S = 1, 1
ATT_HEADS, ATT_KV_HEADS, ATT_HD = 16, 4, 128
ATT_GROUP = ATT_HEADS // ATT_KV_HEADS

VMEM_LIMIT = 56 * 1024 * 1024


def _cp(*sem):
    return pltpu.CompilerParams(dimension_semantics=sem, vmem_limit_bytes=VMEM_LIMIT)


def _sigmoid(x):
    return 1.0 / (1.0 + jnp.exp(-x))


def _rms(x, eps=RMS_EPS):
    return x * lax.rsqrt(jnp.mean(x * x, axis=-1, keepdims=True) + eps)


def _dot(a, b, precision=None):
    return jnp.dot(a, b, preferred_element_type=F32, precision=precision)


def _dot_nt(a, b, precision=None):
    return lax.dot_general(a, b, (((1,), (1,)), ((), ())), preferred_element_type=F32, precision=precision)


def _dot_tn(a, b, precision=None):
    return lax.dot_general(a, b, (((0,), (0,)), ((), ())), preferred_element_type=F32, precision=precision)


def _ada_kernel(c_ref, w_ref, b_ref, o_ref):
    c = c_ref[...]
    o_ref[...] = _dot(c * _sigmoid(c), w_ref[...], HI) + b_ref[...]


def _ada(cond8, ada_w, ada_b):
    depth, d, n = ada_w.shape
    tn = 1024
    return pl.pallas_call(
        _ada_kernel,
        out_shape=jax.ShapeDtypeStruct((depth, 8, n), F32),
        grid=(depth, n // tn),
        in_specs=[pl.BlockSpec((8, d), lambda l, j: (0, 0)),
                  pl.BlockSpec((None, d, tn), lambda l, j: (l, 0, j)),
                  pl.BlockSpec((None, 1, tn), lambda l, j: (l, 0, j))],
        out_specs=pl.BlockSpec((None, 8, tn), lambda l, j: (l, 0, j)),
        compiler_params=_cp("parallel", "parallel"),
        name="ada_mod",
    )(cond8, ada_w, ada_b.reshape(depth, 1, n))


class _Rows:
    def __init__(self, n_p, l_p, n_s, l_s):
        self.n_p, self.l_p, self.n_s, self.l_s = n_p, l_p, n_s, l_s
        self.rows_p = n_p * l_p
        self.rows = self.rows_p + n_s * l_s

    def cond_row(self, i, tm):
        r0 = i * tm
        return jnp.where(r0 < self.rows_p, self.n_s, (r0 - self.rows_p) // self.l_s)


def _mod_spec(rows, tm, layer, piece, d):
    return pl.BlockSpec((None, None, None, 1, d),
                        lambda i, *_: (layer, rows.cond_row(i, tm), piece, 0, 0))


def _nmm_kernel(x_ref, g_ref, sh_ref, sc_ref, w_ref, o_ref, h_ref):
    @pl.when(pl.program_id(1) == 0)
    def _():
        h = _rms(x_ref[...]) * g_ref[...]
        h_ref[...] = (h * (1.0 + sc_ref[...]) + sh_ref[...]).astype(BF16)

    o_ref[...] = _dot(h_ref[...], w_ref[...]).astype(o_ref.dtype)


def _norm_mod_matmul(x, g, mod5, w, rows, layer, pieces, tm, tn, name):
    m, d = x.shape
    n = w.shape[1]
    return pl.pallas_call(
        _nmm_kernel,
        out_shape=jax.ShapeDtypeStruct((m, n), F32),
        grid=(m // tm, n // tn),
        in_specs=[pl.BlockSpec((tm, d), lambda i, j: (i, 0)),
                  pl.BlockSpec((1, d), lambda i, j: (0, 0)),
                  _mod_spec(rows, tm, layer, pieces[0], d),
                  _mod_spec(rows, tm, layer, pieces[1], d),
                  pl.BlockSpec((d, tn), lambda i, j: (0, j))],
        out_specs=pl.BlockSpec((tm, tn), lambda i, j: (i, j)),
        scratch_shapes=[pltpu.VMEM((tm, d), BF16)],
        compiler_params=_cp("parallel", "arbitrary"),
        name=name,
    )(x, g, mod5, mod5, w)


def _out_kernel(mix_ref, w_ref, x_ref, g_ref, gate_ref, o_ref):
    y = _dot(mix_ref[...], w_ref[...])
    o_ref[...] = x_ref[...] + gate_ref[...] * (_rms(y) * g_ref[...])


def _out_proj(mix, w, x, g, mod5, rows, layer, tm, name):
    m, d = x.shape
    k = mix.shape[1]
    return pl.pallas_call(
        _out_kernel,
        out_shape=jax.ShapeDtypeStruct((m, d), F32),
        grid=(m // tm,),
        in_specs=[pl.BlockSpec((tm, k), lambda i: (i, 0)),
                  pl.BlockSpec((k, d), lambda i: (0, 0)),
                  pl.BlockSpec((tm, d), lambda i: (i, 0)),
                  pl.BlockSpec((1, d), lambda i: (0, 0)),
                  _mod_spec(rows, tm, layer, 2, d)],
        out_specs=pl.BlockSpec((tm, d), lambda i: (i, 0)),
        compiler_params=_cp("parallel"),
        name=name,
    )(mix, w, x, g, mod5)


def _ffn_kernel(x_ref, g2_ref, sh_ref, sc_ref, wg_ref, wu_ref, wd_ref, g3_ref, gate_ref, o_ref, h_ref, acc_ref):
    j = pl.program_id(1)

    @pl.when(j == 0)
    def _():
        h = _rms(x_ref[...]) * g2_ref[...]
        h_ref[...] = (h * (1.0 + sc_ref[...]) + sh_ref[...]).astype(BF16)
        acc_ref[...] = jnp.zeros(acc_ref.shape, F32)

    h = h_ref[...]
    a = _dot(h, wg_ref[...])
    t = a * _sigmoid(a) * _dot(h, wu_ref[...])
    acc_ref[...] += _dot(t.astype(BF16), wd_ref[...])

    @pl.when(j == pl.num_programs(1) - 1)
    def _():
        o_ref[...] = x_ref[...] + gate_ref[...] * (_rms(acc_ref[...]) * g3_ref[...])


def _ffn(x, g2, g3, mod5, wg, wu, wd, rows, layer, tm, tf):
    m, d = x.shape
    f = wg.shape[1]
    return pl.pallas_call(
        _ffn_kernel,
        out_shape=jax.ShapeDtypeStruct((m, d), F32),
        grid=(m // tm, f // tf),
        in_specs=[pl.BlockSpec((tm, d), lambda i, j: (i, 0)),
                  pl.BlockSpec((1, d), lambda i, j: (0, 0)),
                  _mod_spec(rows, tm, layer, 3, d),
                  _mod_spec(rows, tm, layer, 4, d),
                  pl.BlockSpec((d, tf), lambda i, j: (0, j)),
                  pl.BlockSpec((d, tf), lambda i, j: (0, j)),
                  pl.BlockSpec((tf, d), lambda i, j: (j, 0)),
                  pl.BlockSpec((1, d), lambda i, j: (0, 0)),
                  _mod_spec(rows, tm, layer, 5, d)],
        out_specs=pl.BlockSpec((tm, d), lambda i, j: (i, 0)),
        scratch_shapes=[pltpu.VMEM((tm, d), BF16), pltpu.VMEM((tm, d), F32)],
        compiler_params=_cp("parallel", "arbitrary"),
        name=f"ffn{layer}",
    )(x, g2, mod5, mod5, wg, wu, wd, g3, mod5)


def _ret_kernel(ld_ref, qf_ref, kf_ref, vf_ref, qb_ref, kb_ref, vb_ref, s0_ref,
                of_ref, ob_ref, sfin_ref, s_scr, mask_scr, xi_scr, zeta_scr):
    c = RET_CHUNK
    b, i = pl.program_id(0), pl.program_id(1)

    @pl.when((b == 0) & (i == 0))
    def _():
        row = lax.broadcasted_iota(jnp.int32, (c, c), 0).astype(F32)
        col = lax.broadcasted_iota(jnp.int32, (c, c), 1).astype(F32)
        for d in range(2):
            diff = row - col if d == 0 else col - row
            pos = row if d == 0 else (c - 1.0) - row
            for h in range(RET_HEADS):
                lg = -jnp.exp(ld_ref[d, h])
                mask_scr[d, h] = jnp.where(diff >= 0, jnp.exp(lg * jnp.maximum(diff, 0.0)), 0.0)
                xi_scr[d, h] = jnp.exp(lg * (pos + 1.0))
                zeta_scr[d, h] = jnp.exp(lg * ((c - 1.0) - pos))

    @pl.when(i == 0)
    def _():
        s_scr[...] = s0_ref[...]

    for d, (q_ref, k_ref, v_ref, o_ref) in enumerate(((qf_ref, kf_ref, vf_ref, of_ref),
                                                      (qb_ref, kb_ref, vb_ref, ob_ref))):
        for h in range(RET_HEADS):
            q = q_ref[:, h * RET_DK:(h + 1) * RET_DK]
            k = k_ref[:, h * RET_DK:(h + 1) * RET_DK] * (RET_DK ** -0.5)
            v = v_ref[:, h * RET_DV:(h + 1) * RET_DV].astype(BF16)
            s = s_scr[d, h]
            qb = q.astype(BF16)
            scores = _dot_nt(qb, k.astype(BF16)) * mask_scr[d, h]
            inner = _dot(scores.astype(BF16), v)
            cross = _dot(qb, s.astype(BF16)) * xi_scr[d, h]
            o_ref[:, h * RET_DV:(h + 1) * RET_DV] = inner + cross
            kz = (k * zeta_scr[d, h][:, :RET_DK]).astype(BF16)
            gamma_c = jnp.exp(-jnp.exp(ld_ref[d, h]) * float(c))
            s_scr[d, h] = gamma_c * s + _dot_tn(kz, v)

    @pl.when(i == pl.num_programs(1) - 1)
    def _():
        sfin_ref[...] = s_scr[...]


def _retention(pa, ld_b, s0, row_off, n_seq, seq_len):
    c = RET_CHUNK
    nc = seq_len // c
    off = row_off // c
    a_qk, a_v = RET_HEADS * RET_DK, RET_HEADS * RET_DV

    def fwd(col):
        return lambda b, i: (off + b * nc + i, col)

    def bwd(col):
        return lambda b, i: (off + b * nc + (nc - 1 - i), col)

    st_spec = pl.BlockSpec((None, 2, RET_HEADS, RET_DK, RET_DV), lambda b, i: (b, 0, 0, 0, 0))
    return pl.pallas_call(
        _ret_kernel,
        out_shape=(jax.ShapeDtypeStruct((n_seq * seq_len, a_v), F32),
                   jax.ShapeDtypeStruct((n_seq * seq_len, a_v), F32),
                   jax.ShapeDtypeStruct((n_seq, 2, RET_HEADS, RET_DK, RET_DV), F32)),
        grid=(n_seq, nc),
        in_specs=[pl.BlockSpec((2, RET_HEADS, 1, c), lambda b, i: (0, 0, 0, 0)),
                  pl.BlockSpec((c, a_qk), fwd(0)), pl.BlockSpec((c, a_qk), fwd(1)), pl.BlockSpec((c, a_v), fwd(1)),
                  pl.BlockSpec((c, a_qk), bwd(0)), pl.BlockSpec((c, a_qk), bwd(1)), pl.BlockSpec((c, a_v), bwd(1)),
                  st_spec],
        out_specs=(pl.BlockSpec((c, a_v), lambda b, i: (b * nc + i, 0)),
                   pl.BlockSpec((c, a_v), lambda b, i: (b * nc + (nc - 1 - i), 0)),
                   st_spec),
        scratch_shapes=[pltpu.VMEM((2, RET_HEADS, RET_DK, RET_DV), F32),
                        pltpu.VMEM((2, RET_HEADS, c, c), F32),
                        pltpu.VMEM((2, RET_HEADS, c, c), F32),
                        pltpu.VMEM((2, RET_HEADS, c, c), F32)],
        compiler_params=_cp("arbitrary", "arbitrary"),
        name="retention",
    )(ld_b, pa, pa, pa, pa, pa, pa, s0)


def _pair_sum(x):
    r = lax.broadcasted_iota(jnp.int32, (128, 128), 0) // RWKV_N
    c = lax.broadcasted_iota(jnp.int32, (128, 128), 1) // RWKV_N
    ones = (r == c).astype(BF16)
    hi, lo = _split(x)
    return _dot(hi, ones) + _dot(lo, ones)


def _prep_kernel(rows, tr, r_ref, k_ref, v_ref, s_ref, rp_ref, kp_ref, vp_ref, sp_ref,
                 rn_ref, kn_ref, vn_ref, sn_ref, cw_ref, w0_ref, wup_ref, a0_ref, aup_ref, gup_ref,
                 kk_ref, ka_ref, rk_ref,
                 r_o, v_o, kkn_o, gb_o, bonus_o, lw_o, kd_o, b_o):
    r0 = pl.program_id(0) * tr
    in_p = r0 < rows.rows_p
    pos = jnp.where(in_p, r0 % rows.l_p, (r0 - rows.rows_p) % rows.l_s)
    seq_len = jnp.where(in_p, rows.l_p, rows.l_s)
    has_prev = (pos != 0).astype(F32)
    has_next = (pos + tr != seq_len).astype(F32)
    rid = lax.broadcasted_iota(jnp.int32, (tr, 1), 0)

    def conv(cur_ref, prev_ref, next_ref, lo, hi):
        cur = cur_ref[...]
        w = cw_ref[:, lo:hi]
        up = jnp.where(rid == 0, prev_ref[7:8, :] * has_prev, pltpu.roll(cur, 1, 0))
        dn = jnp.where(rid == tr - 1, next_ref[0:1, :] * has_next, pltpu.roll(cur, tr - 1, 0))
        return up * w[0:1] + cur * w[1:2] + dn * w[2:3]

    w = RWKV_W
    r = conv(r_ref, rp_ref, rn_ref, 0, w)
    kb = conv(k_ref, kp_ref, kn_ref, w, 2 * w)
    vb = conv(v_ref, vp_ref, vn_ref, 2 * w, 3 * w)
    sm = conv(s_ref, sp_ref, sn_ref, 3 * w, 3 * w + 384)
    gc, wc, ac = sm[:, 0:128], sm[:, 128:256], sm[:, 256:384]

    gb = _mm(_sigmoid(gc), gup_ref[...], _NN, 3)
    twc = jnp.tanh(wc)
    lws, ads = [], []
    for d in range(2):
        lws.append(-DECAY_SCALE * _sigmoid(w0_ref[d:d + 1, :] + _mm(twc, wup_ref[d], _NN, 3)))
        ads.append(_sigmoid(a0_ref[d:d + 1, :] + _mm(ac, aup_ref[d], _NN, 3)))

    for p in range(RWKV_PAIRS):
        sl = slice(p * 128, (p + 1) * 128)
        rp, kp, vp = r[:, sl], kb[:, sl], vb[:, sl]
        kkf = kp * kk_ref[:, sl]
        kkn = kkf / jnp.maximum(jnp.sqrt(_pair_sum(kkf * kkf)), L2_EPS)
        r_o[p] = rp
        v_o[p] = vp
        kkn_o[p] = kkn
        gb_o[p] = gb[:, sl]
        bonus = jnp.zeros_like(rp)
        for d in range(2):
            ad = ads[d][:, sl]
            kd = kp * (1.0 + (ad - 1.0) * ka_ref[:, sl])
            lw_o[d, p] = lws[d][:, sl]
            kd_o[d, p] = kd
            b_o[d, p] = kkn * ad
            bonus = bonus + _pair_sum(rp * kd * rk_ref[:, sl]) * vp
        bonus_o[p] = bonus


def _rwkv_prep(pb, rows, cw, w0, wup, a0, aup, gup, kk, ka, rk, tr):
    m = pb.shape[0]
    w = RWKV_W
    nb8 = m // 8

    def cur(width, col):
        return pl.BlockSpec((tr, width), lambda i: (i, col))

    def prev(width, col):
        return pl.BlockSpec((8, width), lambda i: (jnp.maximum(i * (tr // 8) - 1, 0), col))

    def nxt(width, col):
        return pl.BlockSpec((8, width), lambda i: (jnp.minimum((i + 1) * (tr // 8), nb8 - 1), col))

    def full(a):
        nd = a.ndim
        return pl.BlockSpec(a.shape, lambda i: (0,) * nd)

    pm = jax.ShapeDtypeStruct((RWKV_PAIRS, m, 128), F32)
    pm2 = jax.ShapeDtypeStruct((2, RWKV_PAIRS, m, 128), F32)
    o1 = pl.BlockSpec((RWKV_PAIRS, tr, 128), lambda i: (0, i, 0))
    o2 = pl.BlockSpec((2, RWKV_PAIRS, tr, 128), lambda i: (0, 0, i, 0))
    small_col = (3 * w) // 384
    params = (cw, w0, wup, a0, aup, gup, kk, ka, rk)
    return pl.pallas_call(
        functools.partial(_prep_kernel, rows, tr),
        out_shape=(pm, pm, pm, pm, pm, pm2, pm2, pm2),
        grid=(m // tr,),
        in_specs=[cur(w, 0), cur(w, 1), cur(w, 2), cur(384, small_col),
                  prev(w, 0), prev(w, 1), prev(w, 2), prev(384, small_col),
                  nxt(w, 0), nxt(w, 1), nxt(w, 2), nxt(384, small_col)] + [full(a) for a in params],
        out_specs=(o1, o1, o1, o1, o1, o2, o2, o2),
        compiler_params=_cp("parallel"),
        name="rwkv_prep",
    )(pb, pb, pb, pb, pb, pb, pb, pb, pb, pb, pb, pb, *params)


def _split(x):
    hi = x.astype(BF16)
    return hi, (x - hi.astype(F32)).astype(BF16)


def _mm(a, b, dims, passes):
    def dg(x, y):
        return lax.dot_general(x, y, (dims, ((), ())), preferred_element_type=F32)

    if passes == 1:
        return dg(a.astype(BF16), b.astype(BF16))
    ah, al = _split(a)
    bh, bl = _split(b)
    return dg(ah, bh) + (dg(ah, bl) + dg(al, bh))


_NN = ((1,), (0,))
_NT = ((1,), (1,))
_TN = ((0,), (0,))


def _bd(x):
    first = lax.broadcasted_iota(jnp.int32, x.shape, 1) < RWKV_N
    return jnp.concatenate([jnp.where(first, x, 0.0), jnp.where(first, 0.0, x)], axis=0)


def _scan_pair(d, passes, lw, r, v, kk, kd, b, t_bd):
    c, n = RWKV_CHUNK, RWKV_N
    mm = functools.partial(_mm, passes=passes)
    row = lax.broadcasted_iota(jnp.int32, (c, c), 0)
    col = lax.broadcasted_iota(jnp.int32, (c, c), 1)
    tri = (row >= col) if d == 0 else (row <= col)
    last = c - 1 if d == 0 else 0
    lw_hi, lw_lo = _split(lw)
    tri_b = tri.astype(BF16)
    cum = _dot(tri_b, lw_hi) + _dot(tri_b, lw_lo)
    yield
    tot = cum[last:last + 1, :]
    e_neg = jnp.exp(-cum)
    e_end = jnp.exp(tot - cum)
    kkt = kk * jnp.exp(cum - lw)
    rt = r * jnp.exp(cum)
    kh, bh = kd * e_neg, b * e_neg
    kbar, bbar = kd * e_end, b * e_end

    t4 = lax.broadcasted_iota(jnp.int32, (2 * c, 4 * n), 0)
    s4 = lax.broadcasted_iota(jnp.int32, (2 * c, 4 * n), 1) % n
    before = (s4 < t4 % c) if d == 0 else (s4 > t4 % c)
    keep = before | ((t4 >= c) & (s4 == t4 % c))
    a4 = mm(jnp.concatenate([kkt, rt], axis=0), jnp.concatenate([_bd(bh), _bd(kh)], axis=0), _NT)
    kv = mm(kbar, v, _TN)
    yield
    a4 = jnp.where(keep, a4, 0.0)
    a_b, a_rb = a4[:c, :2 * n], a4[c:, :2 * n]
    av = mm(a4[:, 2 * n:], _bd(v), _NN)
    npow = mm(-a_b, _bd(-a_b), _NN)
    yield
    akv, arkv = av[:c], av[c:]
    eye = (lax.broadcasted_iota(jnp.int32, (c, 2 * n), 0)
           == lax.broadcasted_iota(jnp.int32, (c, 2 * n), 1) % n).astype(F32)
    minv = eye - a_b
    for _ in range(4):
        both = mm(jnp.concatenate([npow, minv], axis=0), _bd(npow), _NN)
        yield
        npow, minv = both[:c], minv + both[c:]
    last_term = mm(minv, _bd(npow), _NN)
    yield
    minv = minv + last_term

    w12 = mm(minv, jnp.concatenate([_bd(kkt), _bd(akv)], axis=1), _NN)
    yield
    w1, w2 = w12[:, :2 * n], w12[:, 2 * n:]
    aw = mm(a_rb, jnp.concatenate([_bd(w1), _bd(w2)], axis=1), _NN)
    gh = mm(bbar, w12, _TN)
    yield
    qy = jnp.concatenate([rt, arkv], axis=1) - aw
    q1, y0 = qy[:, :2 * n], qy[:, 2 * n:]
    r2 = lax.broadcasted_iota(jnp.int32, (2 * n, 2 * n), 0)
    c2 = lax.broadcasted_iota(jnp.int32, (2 * n, 2 * n), 1)
    same_head = (r2 // n) == (c2 // n)
    g_bd = jnp.where(same_head, jnp.where(r2 == c2, jnp.exp(tot), 0.0) - gh[:, :2 * n], 0.0)
    h_bd = jnp.where(same_head, kv - gh[:, 2 * n:], 0.0)
    yt = mm(jnp.concatenate([q1, g_bd], axis=0), t_bd, _NN)
    yield
    return yt[:c] + y0, yt[c:] + h_bd


def _round_robin(gens):
    outs = [None] * len(gens)
    live = list(range(len(gens)))
    while live:
        for k in list(live):
            try:
                next(gens[k])
            except StopIteration as stop:
                outs[k] = stop.value
                live.remove(k)
    return outs


def _scan_kernel(passes, unroll, rf, vf, kkf, lwf, kdf, bf, rb, vb, kkb, lwb, kdb, bb, s0_ref,
                 yf_ref, yb_ref, sfin_ref, t_scr):
    i = pl.program_id(1)
    n = RWKV_N

    @pl.when(i == 0)
    def _():
        z = jnp.zeros((n, n), F32)
        for d in range(2):
            for p in range(RWKV_PAIRS):
                ta, tb = s0_ref[d, 2 * p].T, s0_ref[d, 2 * p + 1].T
                t_scr[d, p] = jnp.concatenate([jnp.concatenate([ta, z], axis=1),
                                               jnp.concatenate([z, tb], axis=1)], axis=0)

    dirs = ((rf, vf, kkf, lwf, kdf, bf, yf_ref), (rb, vb, kkb, lwb, kdb, bb, yb_ref))

    def group(q, carry):
        chains = [(d, q * unroll + u) for u in range(unroll) for d in range(2)]
        args = []
        for d, p in chains:
            r_ref, v_ref, kk_ref, lw_ref, kd_ref, b_ref, _ = dirs[d]
            args.append((lw_ref[p], r_ref[p], v_ref[p], kk_ref[p], kd_ref[p], b_ref[p], t_scr[d, p]))
        outs = _round_robin([_scan_pair(d, passes, *a) for (d, _), a in zip(chains, args)])
        for (d, p), (y, t1) in zip(chains, outs):
            t_scr[d, p] = t1
            dirs[d][6][p] = y
        return carry

    lax.fori_loop(0, RWKV_PAIRS // unroll, group, 0)

    @pl.when(i == pl.num_programs(1) - 1)
    def _():
        for d in range(2):
            for p in range(RWKV_PAIRS):
                t = t_scr[d, p]
                sfin_ref[d, 2 * p] = t[:n, :n].T
                sfin_ref[d, 2 * p + 1] = t[n:, n:].T


def _rwkv_scan(r, v, kk, lw, kd, b, s0, row_off, n_seq, seq_len, passes, unroll):
    c = RWKV_CHUNK
    nc = seq_len // c
    off = row_off // c

    def fwd3(b_, i):
        return (0, off + b_ * nc + i, 0)

    def bwd3(b_, i):
        return (0, off + b_ * nc + (nc - 1 - i), 0)

    def one(index):
        return pl.BlockSpec((RWKV_PAIRS, c, 128), index)

    def per_dir(d, index):
        return pl.BlockSpec((None, RWKV_PAIRS, c, 128), lambda b_, i: (d,) + index(b_, i))

    st_spec = pl.BlockSpec((None, 2, RWKV_HEADS, RWKV_N, RWKV_N), lambda b_, i: (b_, 0, 0, 0, 0))
    ysh = jax.ShapeDtypeStruct((RWKV_PAIRS, n_seq * seq_len, 128), F32)
    return pl.pallas_call(
        functools.partial(_scan_kernel, passes, unroll),
        out_shape=(ysh, ysh, jax.ShapeDtypeStruct((n_seq, 2, RWKV_HEADS, RWKV_N, RWKV_N), F32)),
        grid=(n_seq, nc),
        in_specs=[one(fwd3), one(fwd3), one(fwd3), per_dir(0, fwd3), per_dir(0, fwd3), per_dir(0, fwd3),
                  one(bwd3), one(bwd3), one(bwd3), per_dir(1, bwd3), per_dir(1, bwd3), per_dir(1, bwd3),
                  st_spec],
        out_specs=(pl.BlockSpec((RWKV_PAIRS, c, 128), lambda b_, i: (0, b_ * nc + i, 0)),
                   pl.BlockSpec((RWKV_PAIRS, c, 128), lambda b_, i: (0, b_ * nc + (nc - 1 - i), 0)),
                   st_spec),
        scratch_shapes=[pltpu.VMEM((2, RWKV_PAIRS, 2 * RWKV_N, 2 * RWKV_N), F32)],
        compiler_params=_cp("arbitrary", "arbitrary"),
        name="rwkv_scan",
    )(r, v, kk, lw, kd, b, r, v, kk, lw, kd, b, s0)


def _mix0_kernel(of_ref, ob_ref, g_ref, yf_ref, yb_ref, bonus_ref, gb_ref, lnw_ref, lnb_ref, o_ref):
    for h in range(RET_HEADS):
        sl = slice(h * RET_DV, (h + 1) * RET_DV)
        g = g_ref[:, sl]
        o_ref[:, sl] = (_rms(of_ref[:, sl] + ob_ref[:, sl]) * (g * _sigmoid(g))).astype(o_ref.dtype)
    base = RET_HEADS * RET_DV
    for p in range(RWKV_PAIRS):
        sl = slice(p * 128, (p + 1) * 128)
        y = yf_ref[p] + yb_ref[p]
        mu = _pair_sum(y) * (1.0 / RWKV_N)
        yc = y - mu
        var = _pair_sum(yc * yc) * (1.0 / RWKV_N)
        yn = yc * lax.rsqrt(var + RWKV_LN_EPS) * lnw_ref[:, sl] + lnb_ref[:, sl] + bonus_ref[p]
        o_ref[:, base + p * 128:base + (p + 1) * 128] = (yn * gb_ref[p]).astype(o_ref.dtype)


def _mix0(o_f, o_b, pa, y_f, y_b, bonus, gb, lnw, lnb, tr):
    m = o_f.shape[0]
    a_v = RET_HEADS * RET_DV
    row = pl.BlockSpec((tr, a_v), lambda i: (i, 0))
    pm = pl.BlockSpec((RWKV_PAIRS, tr, 128), lambda i: (0, i, 0))
    vec = pl.BlockSpec((1, RWKV_W), lambda i: (0, 0))
    return pl.pallas_call(
        _mix0_kernel,
        out_shape=jax.ShapeDtypeStruct((m, a_v + RWKV_W), BF16),
        grid=(m // tr,),
        in_specs=[row, row, pl.BlockSpec((tr, a_v), lambda i: (i, 2)), pm, pm, pm, pm, vec, vec],
        out_specs=pl.BlockSpec((tr, a_v + RWKV_W), lambda i: (i, 0)),
        compiler_params=_cp("parallel"),
        name="mix0",
    )(o_f, o_b, pa, y_f, y_b, bonus, gb, lnw, lnb)


def _rot_half(x):
    lane = lax.broadcasted_iota(jnp.int32, x.shape, 1) % 64
    return jnp.where(lane < 32, pltpu.roll(x, 96, 1), pltpu.roll(x, 32, 1))


def _qk_kernel(use_rope, qkv_ref, qg_ref, kg_ref, cos_ref, sin_ref, q_o, k_o, kn_o, v_o):
    hd = ATT_HD
    if use_rope:
        cos, sin = cos_ref[...], sin_ref[...]
    for h in range(ATT_HEADS):
        q = _rms(qkv_ref[:, h * hd:(h + 1) * hd]) * qg_ref[...]
        if use_rope:
            q = q * cos + _rot_half(q) * sin
        q_o[:, h * hd:(h + 1) * hd] = (q * (hd ** -0.5 * LOG2E)).astype(q_o.dtype)
    kbase = ATT_HEADS * hd
    vbase = kbase + ATT_KV_HEADS * hd
    for h in range(ATT_KV_HEADS):
        k = _rms(qkv_ref[:, kbase + h * hd:kbase + (h + 1) * hd]) * kg_ref[...]
        kn_o[:, h * hd:(h + 1) * hd] = k
        if use_rope:
            k = k * cos + _rot_half(k) * sin
        k_o[:, h * hd:(h + 1) * hd] = k.astype(k_o.dtype)
    v_o[...] = qkv_ref[:, vbase:vbase + ATT_KV_HEADS * hd].astype(v_o.dtype)


def _qk_post(qkv, qg, kg, cos, sin, row_off, n_rows, seq_len, use_rope, tr):
    c_mix, kv_w = ATT_HEADS * ATT_HD, ATT_KV_HEADS * ATT_HD
    off = row_off // tr
    per_seq = seq_len // tr
    tab = pl.BlockSpec((tr, ATT_HD), lambda i: (i % per_seq, 0))
    vec = pl.BlockSpec((1, ATT_HD), lambda i: (0, 0))
    return pl.pallas_call(
        functools.partial(_qk_kernel, use_rope),
        out_shape=(jax.ShapeDtypeStruct((n_rows, c_mix), BF16),
                   jax.ShapeDtypeStruct((n_rows, kv_w), BF16),
                   jax.ShapeDtypeStruct((n_rows, kv_w), F32),
                   jax.ShapeDtypeStruct((n_rows, kv_w), BF16)),
        grid=(n_rows // tr,),
        in_specs=[pl.BlockSpec((tr, c_mix + 2 * kv_w), lambda i: (off + i, 0)), vec, vec, tab, tab],
        out_specs=(pl.BlockSpec((tr, c_mix), lambda i: (i, 0)),
                   pl.BlockSpec((tr, kv_w), lambda i: (i, 0)),
                   pl.BlockSpec((tr, kv_w), lambda i: (i, 0)),
                   pl.BlockSpec((tr, kv_w), lambda i: (i, 0))),
        compiler_params=_cp("parallel"),
        name="qk_post",
    )(qkv, qg, kg, cos, sin)


ATT_VROWS = ATT_HD + 8


def _attn_kernel(q_ref, k_ref, vt_ref, o_ref, m_scr, acc_scr):
    j = pl.program_id(3)
    hd = ATT_HD

    @pl.when(j == 0)
    def _():
        m_scr[...] = jnp.full(m_scr.shape, -jnp.inf, F32)
        acc_scr[...] = jnp.zeros(acc_scr.shape, F32)

    k = k_ref[...]
    vt = vt_ref[...]

    def scores(g):
        return _dot_nt(k, q_ref[:, g * hd:(g + 1) * hd])

    st = scores(0)
    for g in range(ATT_GROUP):
        st_next = scores(g + 1) if g + 1 < ATT_GROUP else None
        m_old = m_scr[g]
        m_new = jnp.maximum(m_old, jnp.max(st, axis=0, keepdims=True))
        p = jnp.exp2(st - m_new).astype(BF16)
        acc_scr[g] = jnp.exp2(m_old - m_new) * acc_scr[g] + _dot(vt, p)
        m_scr[g] = m_new
        st = st_next

    @pl.when(j == pl.num_programs(3) - 1)
    def _():
        for g in range(ATT_GROUP):
            acc = acc_scr[g]
            o_ref[:, g * hd:(g + 1) * hd] = (acc[:hd] / acc[hd:hd + 1]).T.astype(o_ref.dtype)


def _attention(q, k_all, vt_all, n_seq, lq, tq, tk):
    lk = k_all.shape[1]
    nq = lq // tq
    gw = ATT_GROUP * ATT_HD
    return pl.pallas_call(
        _attn_kernel,
        out_shape=jax.ShapeDtypeStruct(q.shape, BF16),
        grid=(n_seq, ATT_KV_HEADS, nq, lk // tk),
        in_specs=[pl.BlockSpec((tq, gw), lambda b, h, i, j: (b * nq + i, h)),
                  pl.BlockSpec((None, tk, ATT_HD), lambda b, h, i, j: (b, j, h)),
                  pl.BlockSpec((None, None, ATT_VROWS, tk), lambda b, h, i, j: (b, h, 0, j))],
        out_specs=pl.BlockSpec((tq, gw), lambda b, h, i, j: (b * nq + i, h)),
        scratch_shapes=[pltpu.VMEM((ATT_GROUP, 1, tq), F32),
                        pltpu.VMEM((ATT_GROUP, ATT_VROWS, tq), F32)],
        compiler_params=_cp("parallel", "parallel", "parallel", "arbitrary"),
        name="attention",
    )(q, k_all, vt_all)


def _v_transposed(v):
    n, lk, _ = v.shape
    vt = jnp.swapaxes(v.reshape(n, lk, ATT_KV_HEADS, ATT_HD), 1, 2)
    vt = jnp.swapaxes(vt, 2, 3)
    extra = jnp.zeros((n, ATT_KV_HEADS, ATT_VROWS - ATT_HD, lk), BF16).at[:, :, 0].set(1.0)
    return jnp.concatenate([vt, extra], axis=2)


def _rope_tables(seq_len):
    t = jnp.arange(seq_len, dtype=jnp.int32)
    row = (t // GRID_W).astype(F32)
    col = (t % GRID_W).astype(F32)
    axis_dim = ATT_HD // 2
    freqs = jnp.power(ROPE_THETA, -jnp.arange(0, axis_dim, 2, dtype=F32) / axis_dim)
    ar, ac = row[:, None] * freqs[None, :], col[:, None] * freqs[None, :]
    cos = jnp.concatenate([jnp.cos(ar), jnp.cos(ar), jnp.cos(ac), jnp.cos(ac)], axis=-1)
    sin = jnp.concatenate([-jnp.sin(ar), jnp.sin(ar), -jnp.sin(ac), jnp.sin(ac)], axis=-1)
    return cos, sin


def _pad_cols(w, n):
    return jnp.pad(w, ((0, 0), (0, n - w.shape[1])))


def _pad_rank(w_up):
    z = jnp.zeros_like(w_up[0])
    return jnp.stack([jnp.concatenate([w_up[0], z], axis=0), jnp.concatenate([z, w_up[1]], axis=0)])


def kernel(x_prompt, x_sample, state_ret_fwd, state_ret_bwd, state_rwkv_fwd, state_rwkv_bwd, cache_k, cache_v,
           c, c_ctx, ada_w, ada_b, norm_g, ffn_w_gate, ffn_w_up, ffn_w_down, ab_w_in, ab_w_out, ret_log_decay,
           rwkv_conv_w, rwkv_w0, rwkv_w_up, rwkv_a0, rwkv_a_up, rwkv_g_up, rwkv_k_k, rwkv_k_a, rwkv_r_k,
           rwkv_ln_w, rwkv_ln_b, c_w_in, c_w_out, c_q_norm, c_k_norm):
    n_p, l_p, d = x_prompt.shape
    n_s, l_s, _ = x_sample.shape
    rows = _Rows(n_p, l_p, n_s, l_s)
    rp = rows.rows_p
    a_in = 2 * RET_HEADS * RET_DK + 2 * RET_HEADS * RET_DV
    w = RWKV_W

    x = jnp.concatenate([x_prompt.reshape(rp, d), x_sample.reshape(n_s * l_s, d)], axis=0)
    cond8 = jnp.zeros((8, d), F32).at[:n_s].set(c).at[n_s].set(c_ctx)
    mod5 = _ada(cond8, ada_w, ada_b).reshape(ada_w.shape[0], 8, 6, 1, d)

    tm = 512
    tm_proj = 1024 if rp % 1024 == 0 and l_s % 1024 == 0 else 512
    g0 = norm_g[0]
    w_a = ab_w_in[0][:, :a_in].astype(BF16)
    w_b = _pad_cols(ab_w_in[0][:, a_in:], 3584).astype(BF16)
    pa = _norm_mod_matmul(x, g0[0:1], mod5, w_a, rows, 0, (0, 1), tm_proj, 512, "proj_ret")
    pb = _norm_mod_matmul(x, g0[0:1], mod5, w_b, rows, 0, (0, 1), tm_proj, 512, "proj_rwkv")

    ld_b = jnp.broadcast_to(ret_log_decay[0][:, :, None, None], (2, RET_HEADS, 1, RET_CHUNK))
    zr = jnp.zeros((n_p, 2, RET_HEADS, RET_DK, RET_DV), F32)
    sr = jnp.stack([state_ret_fwd[:, 0], state_ret_bwd[:, 0]], axis=1)
    of_p, ob_p, ret_fin = _retention(pa, ld_b, zr, 0, n_p, l_p)
    of_s, ob_s, _ = _retention(pa, ld_b, sr, rp, n_s, l_s)

    cw = _pad_cols(rwkv_conv_w[0], 3584)
    r_, v_, kk_, gb_, bonus_, lw_, kd_, b_ = _rwkv_prep(
        pb, rows, cw, rwkv_w0[0], _pad_rank(rwkv_w_up[0]), rwkv_a0[0], _pad_rank(rwkv_a_up[0]), rwkv_g_up[0],
        rwkv_k_k[0][None], rwkv_k_a[0][None], rwkv_r_k[0].reshape(1, w), 256)
    zw = jnp.zeros((n_p, 2, RWKV_HEADS, RWKV_N, RWKV_N), F32)
    sw = jnp.stack([state_rwkv_fwd[:, 0], state_rwkv_bwd[:, 0]], axis=1)
    yf_p, yb_p, rwkv_fin = _rwkv_scan(r_, v_, kk_, lw_, kd_, b_, zw, 0, n_p, l_p, SCAN_PASSES_P, 4)
    yf_s, yb_s, _ = _rwkv_scan(r_, v_, kk_, lw_, kd_, b_, sw, rp, n_s, l_s, SCAN_PASSES_S, 4)

    o_f = jnp.concatenate([of_p, of_s], axis=0)
    o_b = jnp.concatenate([ob_p, ob_s], axis=0)
    y_f = jnp.concatenate([yf_p, yf_s], axis=1)
    y_b = jnp.concatenate([yb_p, yb_s], axis=1)
    mix = _mix0(o_f, o_b, pa, y_f, y_b, bonus_, gb_, rwkv_ln_w[0][None], rwkv_ln_b[0][None], 256)
    x = _out_proj(mix, ab_w_out[0].astype(BF16), x, g0[1:2], mod5, rows, 0, tm, "out_proj0")
    x = _ffn(x, g0[2:3], g0[3:4], mod5, ffn_w_gate[0].astype(BF16), ffn_w_up[0].astype(BF16),
             ffn_w_down[0].astype(BF16), rows, 0, tm, 512)

    g1 = norm_g[1]
    kv_w = ATT_KV_HEADS * ATT_HD
    qkv = _norm_mod_matmul(x, g1[0:1], mod5, c_w_in[0].astype(BF16), rows, 1, (0, 1), tm_proj, 512, "proj_qkv")
    cos, sin = _rope_tables(l_s)
    qg, kg = c_q_norm[0][None], c_k_norm[0][None]
    q_p, k_p, kn_p, v_p = _qk_post(qkv, qg, kg, cos, sin, 0, rp, l_p, False, 256)
    q_s, k_s, _, v_s = _qk_post(qkv, qg, kg, cos, sin, rp, n_s * l_s, l_s, True, 256)
    o_p = _attention(q_p, k_p.reshape(n_p, l_p, kv_w), _v_transposed(v_p.reshape(n_p, l_p, kv_w)), n_p, l_p, l_p, l_p)
    past = cache_k.shape[2]
    k_all = jnp.concatenate([cache_k[:, 0].reshape(n_s, past, kv_w).astype(BF16), k_s.reshape(n_s, l_s, kv_w)], axis=1)
    v_all = jnp.concatenate([cache_v[:, 0].reshape(n_s, past, kv_w).astype(BF16), v_s.reshape(n_s, l_s, kv_w)], axis=1)
    o_s = _attention(q_s, k_all, _v_transposed(v_all), n_s, l_s, 512, 512)
    o = jnp.concatenate([o_p, o_s], axis=0)
    x = _out_proj(o, c_w_out[0].astype(BF16), x, g1[1:2], mod5, rows, 1, tm, "out_proj1")
    x = _ffn(x, g1[2:3], g1[3:4], mod5, ffn_w_gate[1].astype(BF16), ffn_w_up[1].astype(BF16),
             ffn_w_down[1].astype(BF16), rows, 1, tm, 512)

    y_prompt = x[:rp].reshape(n_p, l_p, d)
    y_sample = x[rp:].reshape(n_s, l_s, d)
    new_k = kn_p.reshape(n_p, 1, l_p, ATT_KV_HEADS, ATT_HD)
    new_v = qkv[:rp, ATT_HEADS * ATT_HD + kv_w:].reshape(n_p, 1, l_p, ATT_KV_HEADS, ATT_HD)
    return (y_prompt, y_sample, ret_fin[:, 0][:, None], ret_fin[:, 1][:, None],
            rwkv_fin[:, 0][:, None], rwkv_fin[:, 1][:, None], new_k, new_v)
```
